```python
import math
import jax, jax.numpy as jnp
from jax import lax
import numpy as np

D_MODEL = 1024
BATCH = 16
SEQ = 2048
DEPTH = 4

GRID_W = 64
CTX_LEN = 256
N_EVEN = (DEPTH + 1) // 2
N_ODD = DEPTH // 2
EPS = 1e-6
CHUNK = 128
D_A = D_MODEL
A_GROUPS = 8
A_GROUP_DIM = D_A // A_GROUPS
D_B = D_MODEL
HYENA_ORDER = 2
SHORT_CONV = 3
FILTER_EMB = 33
FILTER_BANDS = (FILTER_EMB - 1) // 2
FILTER_HIDDEN = 64
DECAY_TARGET = 1e-2
FAST_DECAY_PCT = 0.3
SLOW_DECAY_PCT = 1.5
MIN_DECAY = math.log(DECAY_TARGET) / SLOW_DECAY_PCT
MAX_DECAY = math.log(DECAY_TARGET) / FAST_DECAY_PCT
D_IN_EVEN = 2 * D_A + (HYENA_ORDER + 1) * D_B
HEAD_DIM = 128
N_HEADS = D_MODEL // HEAD_DIM
N_KV_HEADS = 2
GQA_GROUP = N_HEADS // N_KV_HEADS
Q_BLOCK = 128
ROPE_THETA = 10000.0
D_QKV = (N_HEADS + 2 * N_KV_HEADS) * HEAD_DIM
D_FF = ((8 * D_MODEL + 3 * 256 - 1) // (3 * 256)) * 256

kernel_name = 'hybrid_gmlp_hyena_gqa_prefix_dit'


def rmsnorm(x, g):
    xf = x.astype(jnp.float32)
    y = xf * lax.rsqrt(jnp.mean(xf * xf, axis=-1, keepdims=True) + EPS)
    return (y * g.astype(jnp.float32)).astype(x.dtype)


def layernorm(x, g, b):
    xf = x.astype(jnp.float32)
    mu = jnp.mean(xf, axis=-1, keepdims=True)
    var = jnp.mean(jnp.square(xf - mu), axis=-1, keepdims=True)
    y = (xf - mu) * lax.rsqrt(var + 1e-5)
    return (y * g.astype(jnp.float32) + b.astype(jnp.float32)).astype(x.dtype)


def adaln(cond, w_mod, b_mod):
    return jnp.split(jax.nn.silu(cond) @ w_mod + b_mod, 6, axis=-1)


def modulate(x, g, shift, scale):
    return rmsnorm(x, g) * (1 + scale) + shift


def swiglu(h, w_gu, w_down):
    gate, up = jnp.split(h @ w_gu, 2, axis=-1)
    return (jax.nn.silu(gate) * up) @ w_down


def axial_rope_tables(n):
    rows = n // GRID_W
    row = jnp.repeat(jnp.arange(rows), GRID_W).astype(jnp.float32)
    col = jnp.tile(jnp.arange(GRID_W), rows).astype(jnp.float32)
    half = HEAD_DIM // 2
    inv = ROPE_THETA ** (-jnp.arange(0, half, 2, dtype=jnp.float32) / half)
    ang = jnp.concatenate([row[:, None] * inv, col[:, None] * inv], axis=-1)
    return jnp.cos(ang), jnp.sin(ang)


def apply_rope(x, cos, sin):
    shape = (x.shape[1],) + (1,) * (x.ndim - 3) + (cos.shape[-1],)
    cs, sn = cos.reshape(shape), sin.reshape(shape)
    x1, x2 = jnp.split(x, 2, axis=-1)
    return jnp.concatenate([x1 * cs - x2 * sn, x2 * cs + x1 * sn], axis=-1)


def hyena_filter_freqs(n, w1, b1, w2, b2, w3, freq):
    f32 = jnp.float32
    t = jnp.linspace(0.0, 1.0, n, dtype=f32)[:, None]
    w = (2.0 * math.pi / n) * jnp.arange(n, dtype=f32)[:, None]
    bands = jnp.linspace(1e-4, FILTER_BANDS - 1, FILTER_BANDS, dtype=f32)[None, :]
    ang = bands * w
    z = jnp.concatenate([t, jnp.cos(ang), -jnp.sin(ang)], axis=-1)
    freq = freq.astype(f32)
    hid = jnp.sin(freq[0] * (z @ w1.astype(f32) + b1.astype(f32)))
    hid = jnp.sin(freq[1] * (hid @ w2.astype(f32) + b2.astype(f32)))
    h = (hid @ w3.astype(f32)).reshape(n, 2, HYENA_ORDER, D_B)
    deltas = jnp.abs(jnp.linspace(MIN_DECAY, MAX_DECAY, D_B, dtype=f32))
    h = h * jnp.exp(-t[:, :, None, None] * deltas)
    h_circ = jnp.concatenate(
        [h[:, 0], jnp.zeros((1, HYENA_ORDER, D_B), f32), h[:0:-1, 1]], axis=0)
    h_circ = h_circ / jnp.sum(jnp.abs(h_circ), axis=0, keepdims=True)
    return jnp.fft.rfft(h_circ, axis=0)


def bidir_long_conv(z, h_freq, skip):
    n = z.shape[1]
    zf = z.astype(jnp.float32)
    y = jnp.fft.irfft(jnp.fft.rfft(zf, n=2 * n, axis=1) * h_freq, n=2 * n, axis=1)[:, :n]
    return (y + zf * skip.astype(jnp.float32)).astype(z.dtype)


def centred_short_conv(z, w, b):
    n = z.shape[1]
    pad = SHORT_CONV // 2
    zp = jnp.pad(z, ((0, 0), (pad, pad), (0, 0)))
    return sum(zp[:, k:k + n] * w[k] for k in range(SHORT_CONV)) + b


def gmlp_hyena_mixer(h, h_freq, w_in, ln_g, ln_b, w_s, b_s, conv_w, conv_b, skip, w_out):
    bsz, n, _ = h.shape
    proj = h @ w_in
    u, v = jnp.split(jax.nn.gelu(proj[..., :2 * D_A], approximate=False), 2, axis=-1)
    v = layernorm(v, ln_g, ln_b).reshape(bsz, n // CHUNK, CHUNK, A_GROUPS, A_GROUP_DIM)
    sv = jnp.einsum('gpq,bkqgc->bkpgc', w_s, v) + b_s.T[None, None, :, :, None]
    y_a = u * sv.reshape(bsz, n, D_A)
    hb = centred_short_conv(proj[..., 2 * D_A:], conv_w, conv_b)
    vb, x1, x2 = jnp.split(hb, 3, axis=-1)
    z = x1 * bidir_long_conv(vb, h_freq[:, 0], skip[0])
    z = x2 * bidir_long_conv(z, h_freq[:, 1], skip[1])
    return jnp.concatenate([y_a, z], axis=-1) @ w_out


def gqa_block(q, k, v):
    s = jnp.einsum('bqkgd,bskd->bkgqs', q, k) * (HEAD_DIM ** -0.5)
    p = jax.nn.softmax(s, axis=-1)
    return jnp.einsum('bkgqs,bskd->bqkgd', p, v)


def attention_mixer(h_lat, h_ctx, w_qkv, q_g, k_g, w_o, cos, sin, need_ctx_out):
    def project(h, with_q):
        bsz, n, _ = h.shape
        nq = N_HEADS * HEAD_DIM
        nk = N_KV_HEADS * HEAD_DIM
        w = w_qkv if with_q else w_qkv[:, nq:]
        qkv = h @ w
        off = nq if with_q else 0
        k = rmsnorm(qkv[..., off:off + nk].reshape(bsz, n, N_KV_HEADS, HEAD_DIM), k_g).astype(jnp.float32)
        v = qkv[..., off + nk:].reshape(bsz, n, N_KV_HEADS, HEAD_DIM).astype(jnp.float32)
        q = None
        if with_q:
            q = rmsnorm(qkv[..., :nq].reshape(bsz, n, N_KV_HEADS, GQA_GROUP, HEAD_DIM), q_g).astype(jnp.float32)
        return q, k, v

    bsz, n, _ = h_lat.shape
    q_l, k_l, v_l = project(h_lat, True)
    q_l, k_l = apply_rope(q_l, cos, sin), apply_rope(k_l, cos, sin)
    q_c, k_c, v_c = project(h_ctx, need_ctx_out)
    k_all = jnp.concatenate([k_c, k_l], axis=1)
    v_all = jnp.concatenate([v_c, v_l], axis=1)
    n_blk = n // Q_BLOCK
    q_blocks = q_l.reshape(bsz, n_blk, Q_BLOCK, N_KV_HEADS, GQA_GROUP, HEAD_DIM).swapaxes(0, 1)
    o_l = lax.map(lambda qb: gqa_block(qb, k_all, v_all), q_blocks)
    o_l = o_l.swapaxes(0, 1).reshape(bsz, n, N_HEADS * HEAD_DIM).astype(h_lat.dtype) @ w_o
    o_c = None
    if need_ctx_out:
        n_c = h_ctx.shape[1]
        o_c = gqa_block(q_c, k_c, v_c).reshape(bsz, n_c, N_HEADS * HEAD_DIM).astype(h_ctx.dtype) @ w_o
    return o_l, o_c


def setup_inputs(seed: int = 0) -> dict:
    key = jax.random.key(seed)
    ks = iter(jax.random.split(key, 40))
    f32 = jnp.float32

    def nrm(shape, scale):
        return jax.random.normal(next(ks), shape, f32) * scale

    def gain(shape):
        return 1.0 + nrm(shape, 0.01)

    return {
        'x': nrm((BATCH, SEQ, D_MODEL), 1.0),
        'c': nrm((BATCH, D_MODEL), 1.0),
        'ctx': nrm((BATCH, CTX_LEN, D_MODEL), 1.0),
        'c_ctx': nrm((D_MODEL,), 1.0),
        'mod_w': nrm((DEPTH, D_MODEL, 6 * D_MODEL), 0.5 * D_MODEL ** -0.5),
        'mod_b': nrm((DEPTH, 6 * D_MODEL), 0.01),
        'norm1_g': gain((DEPTH, D_MODEL)),
        'norm2_g': gain((DEPTH, D_MODEL)),
        'ffn_w_gu': nrm((DEPTH, D_MODEL, 2 * D_FF), D_MODEL ** -0.5),
        'ffn_w_down': nrm((DEPTH, D_FF, D_MODEL), D_FF ** -0.5),
        'even_w_in': nrm((N_EVEN, D_MODEL, D_IN_EVEN), D_MODEL ** -0.5),
        'gmlp_ln_g': gain((N_EVEN, D_A)),
        'gmlp_ln_b': nrm((N_EVEN, D_A), 0.01),
        'gmlp_w_s': nrm((N_EVEN, A_GROUPS, CHUNK, CHUNK), CHUNK ** -0.5),
        'gmlp_b_s': gain((N_EVEN, A_GROUPS, CHUNK)),
        'hyena_conv_w': nrm((N_EVEN, SHORT_CONV, (HYENA_ORDER + 1) * D_B), SHORT_CONV ** -0.5),
        'hyena_conv_b': nrm((N_EVEN, (HYENA_ORDER + 1) * D_B), 0.01),
        'hyena_f_w1': nrm((N_EVEN, FILTER_EMB, FILTER_HIDDEN), FILTER_EMB ** -0.5),
        'hyena_f_b1': nrm((N_EVEN, FILTER_HIDDEN), 0.1),
        'hyena_f_w2': nrm((N_EVEN, FILTER_HIDDEN, FILTER_HIDDEN), FILTER_HIDDEN ** -0.5),
        'hyena_f_b2': nrm((N_EVEN, FILTER_HIDDEN), 0.1),
        'hyena_f_w3': nrm((N_EVEN, FILTER_HIDDEN, 2 * HYENA_ORDER * D_B), FILTER_HIDDEN ** -0.5),
        'hyena_freq': gain((N_EVEN, 2, FILTER_HIDDEN)),
        'hyena_skip': nrm((N_EVEN, HYENA_ORDER, D_B), 0.1),
        'even_w_out': nrm((N_EVEN, D_A + D_B, D_MODEL), (D_A + D_B) ** -0.5),
        'attn_w_qkv': nrm((N_ODD, D_MODEL, D_QKV), D_MODEL ** -0.5),
        'attn_q_g': gain((N_ODD, HEAD_DIM)),
        'attn_k_g': gain((N_ODD, HEAD_DIM)),
        'attn_w_o': nrm((N_ODD, N_HEADS * HEAD_DIM, D_MODEL), (N_HEADS * HEAD_DIM) ** -0.5),
        'final_g': gain((D_MODEL,)),
    }


def reference(x, c, ctx, c_ctx, mod_w, mod_b, norm1_g, norm2_g, ffn_w_gu, ffn_w_down,
              even_w_in, gmlp_ln_g, gmlp_ln_b, gmlp_w_s, gmlp_b_s, hyena_conv_w, hyena_conv_b,
              hyena_f_w1, hyena_f_b1, hyena_f_w2, hyena_f_b2, hyena_f_w3, hyena_freq, hyena_skip,
              even_w_out, attn_w_qkv, attn_q_g, attn_k_g, attn_w_o, final_g):
    n = x.shape[1]
    n_ctx = ctx.shape[1]
    cos, sin = axial_rope_tables(n)
    for layer in range(DEPTH):
        last = layer == DEPTH - 1
        is_even = layer % 2 == 0
        ctx_needed = (not last) or (not is_even)
        sh1, sc1, g1, sh2, sc2, g2 = adaln(c[:, None, :], mod_w[layer], mod_b[layer])
        h = modulate(x, norm1_g[layer], sh1, sc1)
        if ctx_needed:
            csh1, csc1, cg1, csh2, csc2, cg2 = adaln(c_ctx, mod_w[layer], mod_b[layer])
            hc = modulate(ctx, norm1_g[layer], csh1, csc1)
        if is_even:
            i = layer // 2
            filt = (hyena_f_w1[i], hyena_f_b1[i], hyena_f_w2[i], hyena_f_b2[i], hyena_f_w3[i], hyena_freq[i])
            prm = (even_w_in[i], gmlp_ln_g[i], gmlp_ln_b[i], gmlp_w_s[i], gmlp_b_s[i],
                   hyena_conv_w[i], hyena_conv_b[i], hyena_skip[i], even_w_out[i])
            y = gmlp_hyena_mixer(h, hyena_filter_freqs(n, *filt), *prm)
            yc = gmlp_hyena_mixer(hc, hyena_filter_freqs(n_ctx, *filt), *prm) if not last else None
        else:
            j = layer // 2
            y, yc = attention_mixer(h, hc, attn_w_qkv[j], attn_q_g[j], attn_k_g[j], attn_w_o[j],
                                    cos, sin, not last)
        x = x + g1 * y
        x = x + g2 * swiglu(modulate(x, norm2_g[layer], sh2, sc2), ffn_w_gu[layer], ffn_w_down[layer])
        if not last:
            ctx = ctx + cg1 * yc
            ctx = ctx + cg2 * swiglu(modulate(ctx, norm2_g[layer], csh2, csc2),
                                     ffn_w_gu[layer], ffn_w_down[layer])
    return rmsnorm(x, final_g)
```

```python
import functools
import math

import numpy as np
import jax
import jax.numpy as jnp
from jax import lax
from jax.experimental import pallas as pl
from jax.experimental.pallas import tpu as pltpu

D = 1024
NB = 16
SEQ = 2048
CTX = 256
DEPTH = 4
GRID_W = 64
EPS = 1e-6
CHUNK = 128
A_GROUPS = 8
HEAD_DIM = 128
N_HEADS = 8
N_KV = 2
GQA = N_HEADS // N_KV
D_QKV = (N_HEADS + 2 * N_KV) * HEAD_DIM
D_FF = 2816
FILTER_EMB = 33
FILTER_BANDS = 16
FILTER_HIDDEN = 64
MIN_DECAY = math.log(1e-2) / 1.5
MAX_DECAY = math.log(1e-2) / 0.3
ROPE_THETA = 10000.0

CTX_ROW = NB
MOD_ROWS = 24
HY_P = 256
HY_TC = 128
VMEM_LIMIT = 48 * 1024 * 1024

bf16 = jnp.bfloat16
f32 = jnp.float32


def _mm(a, b):
    return jnp.dot(a.astype(bf16), b.astype(bf16), preferred_element_type=f32)


def _mm_f32(a, b):
    return jnp.dot(a, b, preferred_element_type=f32, precision=lax.Precision.HIGHEST)


def _params(sem):
    return pltpu.CompilerParams(dimension_semantics=sem, vmem_limit_bytes=VMEM_LIMIT)


def _mods_kernel(cond_ref, w_ref, b_ref, o_ref):
    a = jax.nn.silu(cond_ref[...])
    o_ref[0] = _mm(a, w_ref[0]) + b_ref[0]


def _mods(cond, mod_w, mod_b):
    tn = 1024
    n6 = 6 * D
    return pl.pallas_call(
        _mods_kernel,
        grid=(DEPTH, n6 // tn),
        in_specs=[
            pl.BlockSpec((MOD_ROWS, D), lambda l, j: (0, 0)),
            pl.BlockSpec((1, D, tn), lambda l, j: (l, 0, j)),
            pl.BlockSpec((1, 1, tn), lambda l, j: (l, 0, j)),
        ],
        out_specs=pl.BlockSpec((1, MOD_ROWS, tn), lambda l, j: (l, 0, j)),
        out_shape=jax.ShapeDtypeStruct((DEPTH, MOD_ROWS, n6), f32),
        compiler_params=_params(("arbitrary", "arbitrary")),
    )(cond, mod_w, mod_b.reshape(DEPTH, 1, n6))


def _mod_spec(row_of, chunk, nidx):
    if nidx == 2:
        return pl.BlockSpec((1, 1, D), lambda b, i: (row_of(b), 0, chunk))
    return pl.BlockSpec((1, 1, D), lambda b, i, j: (row_of(b), 0, chunk))


def _norm_mod(x, g, sc, sh):
    y = x * lax.rsqrt(jnp.mean(x * x, axis=-1, keepdims=True) + EPS)
    return (y * g) * (1.0 + sc) + sh


def _nmm_kernel(x_ref, g_ref, sh_ref, sc_ref, w_ref, o_ref, h_ref):
    @pl.when(pl.program_id(2) == 0)
    def _():
        h_ref[...] = _norm_mod(x_ref[0], g_ref[...], sc_ref[0], sh_ref[0]).astype(bf16)

    o_ref[0] = jnp.dot(h_ref[...], w_ref[...], preferred_element_type=f32)


def _norm_mod_matmul(x, g, mods3, row_of, w, tm, tn):
    b, n, _ = x.shape
    nout = w.shape[1]
    return pl.pallas_call(
        _nmm_kernel,
        grid=(b, n // tm, nout // tn),
        in_specs=[
            pl.BlockSpec((1, tm, D), lambda b, i, j: (b, i, 0)),
            pl.BlockSpec((1, D), lambda b, i, j: (0, 0)),
            _mod_spec(row_of, 0, 3),
            _mod_spec(row_of, 1, 3),
            pl.BlockSpec((D, tn), lambda b, i, j: (0, j)),
        ],
        out_specs=pl.BlockSpec((1, tm, tn), lambda b, i, j: (b, i, j)),
        out_shape=jax.ShapeDtypeStruct((b, n, nout), f32),
        scratch_shapes=[pltpu.VMEM((tm, D), bf16)],
        compiler_params=_params(("arbitrary", "arbitrary", "arbitrary")),
    )(x, g, mods3, mods3, w)


def _gelu(x):
    return 0.5 * x * (1.0 + lax.erf(x * (2.0 ** -0.5)))


def _gmlp_kernel(u_ref, v_ref, lng_ref, lnb_ref, ws_ref, bs_ref, o_ref, *, tm):
    u = _gelu(u_ref[0])
    v = _gelu(v_ref[0])
    mu = jnp.mean(v, axis=-1, keepdims=True)
    vc = v - mu
    var = jnp.mean(vc * vc, axis=-1, keepdims=True)
    vn = (vc * lax.rsqrt(var + 1e-5) * lng_ref[...] + lnb_ref[...]).astype(bf16)
    for k in range(tm // CHUNK):
        rows = slice(k * CHUNK, (k + 1) * CHUNK)
        for g in range(A_GROUPS):
            cols = slice(g * CHUNK, (g + 1) * CHUNK)
            sv = jnp.dot(ws_ref[g], vn[rows, cols], preferred_element_type=f32)
            o_ref[0, rows, cols] = (u[rows, cols] * (sv + bs_ref[:, cols])).astype(bf16)


def _gmlp(proj, ln_g, ln_b, w_s, b_s_rows, tm):
    b, n, _ = proj.shape
    return pl.pallas_call(
        functools.partial(_gmlp_kernel, tm=tm),
        grid=(b, n // tm),
        in_specs=[
            pl.BlockSpec((1, tm, D), lambda b, i: (b, i, 0)),
            pl.BlockSpec((1, tm, D), lambda b, i: (b, i, 1)),
            pl.BlockSpec((1, D), lambda b, i: (0, 0)),
            pl.BlockSpec((1, D), lambda b, i: (0, 0)),
            pl.BlockSpec((A_GROUPS, CHUNK, CHUNK), lambda b, i: (0, 0, 0)),
            pl.BlockSpec((CHUNK, D), lambda b, i: (0, 0)),
        ],
        out_specs=pl.BlockSpec((1, tm, D), lambda b, i: (b, i, 0)),
        out_shape=jax.ShapeDtypeStruct((b, n, D), bf16),
        compiler_params=_params(("arbitrary", "arbitrary")),
    )(proj, proj, ln_g, ln_b, w_s, b_s_rows)


def _filter_consts(n):
    lag = np.abs(np.arange(2 * n) - n)
    lag[0] = 0
    t = np.linspace(0.0, 1.0, n)[lag][:, None]
    w = (2.0 * math.pi / n) * lag[:, None]
    bands = np.linspace(1e-4, FILTER_BANDS - 1, FILTER_BANDS)[None, :]
    ang = bands * w
    z = np.concatenate([t, np.cos(ang), -np.sin(ang)], axis=-1)
    zp = np.zeros((2 * n, 128), np.float64)
    zp[:, :FILTER_EMB] = z
    deltas = np.abs(np.linspace(MIN_DECAY, MAX_DECAY, D))
    return jnp.asarray(zp, f32), jnp.asarray(np.tile(deltas, 2)[None, :], f32)


def _filter_time_kernel(z_ref, w1_ref, b1_ref, w2_ref, b2_ref, f0_ref, f1_ref, w3_ref, dl_ref,
                        h_ref, s_ref, *, tr):
    r = pl.program_id(0)
    z = z_ref[...]
    hid = jnp.sin(f0_ref[...] * (_mm_f32(z, w1_ref[...]) + b1_ref[...]))
    hid = jnp.sin(f1_ref[...] * (_mm_f32(hid, w2_ref[...]) + b2_ref[...]))
    h = _mm_f32(hid, w3_ref[0])
    h = h * jnp.exp(-z[:, 0:1] * dl_ref[...])
    row = lax.broadcasted_iota(jnp.int32, h.shape, 0) + r * tr
    h = jnp.where(row == 0, 0.0, h)
    h_ref[...] = h

    @pl.when(r == 0)
    def _():
        s_ref[...] = jnp.zeros_like(s_ref)

    s_ref[...] += jnp.sum(jnp.abs(h), axis=0, keepdims=True)


def _filter_time(n, zemb, deltas, w1p, b1p, w2p, b2p, f0p, f1p, w3p):
    tr = 256
    half = n // tr
    c2 = 2 * D
    small = lambda shape: pl.BlockSpec(shape, lambda r: (0,) * len(shape))
    return pl.pallas_call(
        functools.partial(_filter_time_kernel, tr=tr),
        grid=(2 * n // tr,),
        in_specs=[
            pl.BlockSpec((tr, 128), lambda r: (r, 0)),
            small((128, 128)), small((1, 128)), small((128, 128)), small((1, 128)),
            small((1, 128)), small((1, 128)),
            pl.BlockSpec((1, 128, c2), lambda r: (jnp.where(r < half, 1, 0), 0, 0)),
            small((1, c2)),
        ],
        out_specs=[pl.BlockSpec((tr, c2), lambda r: (r, 0)), small((1, c2))],
        out_shape=[jax.ShapeDtypeStruct((2 * n, c2), f32), jax.ShapeDtypeStruct((1, c2), f32)],
        compiler_params=_params(("arbitrary",)),
    )(zemb, w1p, b1p, w2p, b2p, f0p, f1p, w3p, deltas)


def _dft_consts(p):
    k = np.arange(p)
    theta = np.pi * (2 * k + 1) / (2 * p)
    s = np.arange(p)
    c = np.cos(theta[:, None] * s[None, :])
    sn = np.sin(theta[:, None] * s[None, :])
    ffwd = np.concatenate([c, -sn], axis=0)
    finv = np.concatenate([c.T, -sn.T], axis=1) / p
    m = np.arange(-p, p)
    g = np.concatenate([np.cos(theta[:, None] * m[None, :]),
                        -np.sin(theta[:, None] * m[None, :])], axis=0)
    g[:, 0] = 0.0
    return jnp.asarray(ffwd, bf16), jnp.asarray(finv, bf16), jnp.asarray(g, f32)


def _filter_spec_kernel(g_ref, lo_ref, hi_ref, s_ref, ha_ref, hb_ref, *, p):
    h = _mm_f32(g_ref[:, :p], lo_ref[...]) + _mm_f32(g_ref[:, p:], hi_ref[...])
    h = h / s_ref[...]
    ha_ref[0] = h[:p]
    hb_ref[0] = h[p:]


def _filter_spec(n, p, gmat, hraw, hsum):
    nb = n // p
    c2 = 2 * D
    tc = 512
    nd = 2 * nb - 1
    out = jax.ShapeDtypeStruct((nd, p, c2), f32)
    return pl.pallas_call(
        functools.partial(_filter_spec_kernel, p=p),
        grid=(nd, c2 // tc),
        in_specs=[
            pl.BlockSpec((2 * p, 2 * p), lambda d, c: (0, 0)),
            pl.BlockSpec((p, tc), lambda d, c: (d, c)),
            pl.BlockSpec((p, tc), lambda d, c: (d + 1, c)),
            pl.BlockSpec((1, tc), lambda d, c: (0, c)),
        ],
        out_specs=[pl.BlockSpec((1, p, tc), lambda d, c: (d, 0, c))] * 2,
        out_shape=[out, out],
        compiler_params=_params(("arbitrary", "arbitrary")),
    )(gmat, hraw, hraw, hsum)


HY_ROWS = 64


def _hyena_kernel(pv_ref, p1_ref, p2_ref, cw_ref, cb_ref, skip_ref,
                  ha0_ref, hb0_ref, ha1_ref, hb1_ref, ffwd_ref, finv_ref,
                  o_ref, sv_ref, s1_ref, s2_ref, u_ref, za_ref, zb_ref, yc_ref, *, n, p):
    nb = n // p
    tc = o_ref.shape[-1]
    row = lax.broadcasted_iota(jnp.int32, (n, tc), 0)

    def short_conv(p_ref, part, dst_ref):
        x = p_ref[0]
        prev = jnp.where(row == 0, 0.0, pltpu.roll(x, 1, axis=0))
        nxt = jnp.where(row == n - 1, 0.0, pltpu.roll(x, n - 1, axis=0))
        w = cw_ref[part]
        dst_ref[...] = prev * w[0:1] + x * w[1:2] + nxt * w[2:3] + cb_ref[part]

    short_conv(pv_ref, 0, sv_ref)
    short_conv(p1_ref, 1, s1_ref)
    short_conv(p2_ref, 2, s2_ref)

    def long_conv(src_ref, gate_ref, ha_ref, hb_ref, skip, store):
        for j in range(nb):
            zt = jnp.dot(ffwd_ref[...], src_ref[j * p:(j + 1) * p, :].astype(bf16),
                         preferred_element_type=f32)
            za_ref[j] = zt[:p]
            zb_ref[j] = zt[p:]

        def out_block(i):
            for c in range(p // HY_ROWS):
                rows = pl.ds(c * HY_ROWS, HY_ROWS)
                ya = jnp.zeros((HY_ROWS, tc), f32)
                yb = jnp.zeros((HY_ROWS, tc), f32)
                for j in range(nb):
                    d = i - j + (nb - 1)
                    h_a = ha_ref[d, rows, :]
                    h_b = hb_ref[d, rows, :]
                    z_a = za_ref[j, rows, :]
                    z_b = zb_ref[j, rows, :]
                    ya = ya + h_a * z_a - h_b * z_b
                    yb = yb + h_a * z_b + h_b * z_a
                yc_ref[pl.ds(c * HY_ROWS, HY_ROWS), :] = ya.astype(bf16)
                yc_ref[pl.ds(p + c * HY_ROWS, HY_ROWS), :] = yb.astype(bf16)
            y = jnp.dot(finv_ref[...], yc_ref[...], preferred_element_type=f32)
            r0 = i * p if isinstance(i, int) else pl.multiple_of(i * p, p)
            src = src_ref[pl.ds(r0, p), :]
            store(r0, gate_ref[pl.ds(r0, p), :] * (y + src * skip))

        if nb == 1:
            out_block(0)
        else:
            def body(i, carry):
                out_block(i)
                return carry
            lax.fori_loop(0, nb, body, 0)

    def store_u(r0, val):
        u_ref[pl.ds(r0, p), :] = val

    def store_o(r0, val):
        o_ref[0, pl.ds(r0, p), :] = val.astype(o_ref.dtype)

    long_conv(sv_ref, s1_ref, ha0_ref, hb0_ref, skip_ref[0:1, :], store_u)
    long_conv(u_ref, s2_ref, ha1_ref, hb1_ref, skip_ref[1:2, :], store_o)


def _hyena(proj, conv_w3, conv_b3, skip, h_a, h_b, ffwd, finv, p):
    b, n, _ = proj.shape
    tc = HY_TC
    nct = D // tc
    nb = n // p
    nd = 2 * nb - 1
    col0 = 2 * D // tc

    def pspec(part):
        return pl.BlockSpec((1, n, tc), lambda c, b: (b, 0, col0 + part * nct + c))

    def hspec(order):
        return pl.BlockSpec((nd, p, tc), lambda c, b: (0, 0, order * nct + c))

    return pl.pallas_call(
        functools.partial(_hyena_kernel, n=n, p=p),
        grid=(nct, b),
        in_specs=[
            pspec(0), pspec(1), pspec(2),
            pl.BlockSpec((3, 3, tc), lambda c, b: (0, 0, c)),
            pl.BlockSpec((3, 1, tc), lambda c, b: (0, 0, c)),
            pl.BlockSpec((2, tc), lambda c, b: (0, c)),
            hspec(0), hspec(0), hspec(1), hspec(1),
            pl.BlockSpec((2 * p, p), lambda c, b: (0, 0)),
            pl.BlockSpec((p, 2 * p), lambda c, b: (0, 0)),
        ],
        out_specs=pl.BlockSpec((1, n, tc), lambda c, b: (b, 0, c)),
        out_shape=jax.ShapeDtypeStruct((b, n, D), bf16),
        scratch_shapes=[
            pltpu.VMEM((n, tc), f32), pltpu.VMEM((n, tc), f32), pltpu.VMEM((n, tc), f32),
            pltpu.VMEM((n, tc), f32),
            pltpu.VMEM((nb, p, tc), f32), pltpu.VMEM((nb, p, tc), f32),
            pltpu.VMEM((2 * p, tc), bf16),
        ],
        compiler_params=_params(("arbitrary", "arbitrary")),
    )(proj, proj, proj, conv_w3, conv_b3, skip, h_a, h_b, h_a, h_b, ffwd, finv)


def _proj_res_kernel(*refs, na):
    x_ref, gate_ref = refs[0], refs[1]
    a_refs = refs[2:2 + na]
    w_refs = refs[2 + na:2 + 2 * na]
    o_ref = refs[2 + 2 * na]
    acc = jnp.dot(a_refs[0][0], w_refs[0][...], preferred_element_type=f32)
    for k in range(1, na):
        acc = acc + jnp.dot(a_refs[k][0], w_refs[k][...], preferred_element_type=f32)
    o_ref[0] = x_ref[0] + gate_ref[0] * acc


def _proj_residual(x, mods3, row_of, gate_chunk, acts, ws, tm):
    b, n, _ = x.shape
    na = len(acts)
    return pl.pallas_call(
        functools.partial(_proj_res_kernel, na=na),
        grid=(b, n // tm),
        in_specs=(
            [pl.BlockSpec((1, tm, D), lambda b, i: (b, i, 0)), _mod_spec(row_of, gate_chunk, 2)]
            + [pl.BlockSpec((1, tm, a.shape[-1]), lambda b, i: (b, i, 0)) for a in acts]
            + [pl.BlockSpec(w.shape, lambda b, i: (0, 0)) for w in ws]
        ),
        out_specs=pl.BlockSpec((1, tm, D), lambda b, i: (b, i, 0)),
        out_shape=jax.ShapeDtypeStruct((b, n, D), f32),
        compiler_params=_params(("arbitrary", "arbitrary")),
    )(x, mods3, *acts, *ws)


def _ffn_kernel(x_ref, g_ref, sh_ref, sc_ref, gate_ref, wg_ref, wu_ref, wd_ref, o_ref,
                h_ref, acc_ref):
    f = pl.program_id(2)

    @pl.when(f == 0)
    def _():
        h_ref[...] = _norm_mod(x_ref[0], g_ref[...], sc_ref[0], sh_ref[0]).astype(bf16)
        acc_ref[...] = jnp.zeros_like(acc_ref)

    h = h_ref[...]
    gt = jnp.dot(h, wg_ref[...], preferred_element_type=f32)
    up = jnp.dot(h, wu_ref[...], preferred_element_type=f32)
    act = (jax.nn.silu(gt) * up).astype(bf16)
    acc_ref[...] += jnp.dot(act, wd_ref[...], preferred_element_type=f32)

    @pl.when(f == pl.num_programs(2) - 1)
    def _():
        o_ref[0] = x_ref[0] + gate_ref[0] * acc_ref[...]


def _ffn(x, g, mods3, row_of, w_gu, w_down, tm, tf):
    b, n, _ = x.shape
    nf = D_FF // tf
    return pl.pallas_call(
        _ffn_kernel,
        grid=(b, n // tm, nf),
        in_specs=[
            pl.BlockSpec((1, tm, D), lambda b, i, j: (b, i, 0)),
            pl.BlockSpec((1, D), lambda b, i, j: (0, 0)),
            _mod_spec(row_of, 3, 3),
            _mod_spec(row_of, 4, 3),
            _mod_spec(row_of, 5, 3),
            pl.BlockSpec((D, tf), lambda b, i, j: (0, j)),
            pl.BlockSpec((D, tf), lambda b, i, j: (0, nf + j)),
            pl.BlockSpec((tf, D), lambda b, i, j: (j, 0)),
        ],
        out_specs=pl.BlockSpec((1, tm, D), lambda b, i, j: (b, i, 0)),
        out_shape=jax.ShapeDtypeStruct((b, n, D), f32),
        scratch_shapes=[pltpu.VMEM((tm, D), bf16), pltpu.VMEM((tm, D), f32)],
        compiler_params=_params(("arbitrary", "arbitrary", "arbitrary")),
    )(x, g, mods3, mods3, mods3, w_gu, w_gu, w_down)


def _rope_tables(n):
    rows = n // GRID_W
    row = np.repeat(np.arange(rows), GRID_W).astype(np.float64)
    col = np.tile(np.arange(GRID_W), rows).astype(np.float64)
    half = HEAD_DIM // 2
    inv = ROPE_THETA ** (-np.arange(0, half, 2, dtype=np.float64) / half)
    ang = np.concatenate([row[:, None] * inv, col[:, None] * inv], axis=-1)
    cos, sin = np.cos(ang), np.sin(ang)
    return (jnp.asarray(np.concatenate([cos, cos], axis=-1), f32),
            jnp.asarray(np.concatenate([-sin, sin], axis=-1), f32))


def _qkv_kernel(*refs, with_q, rope):
    x_ref, g_ref, sh_ref, sc_ref, w_ref, qg_ref, kg_ref = refs[:7]
    pos = 7
    if rope:
        cos_ref, sin_ref = refs[7:9]
        pos = 9
    out_refs = refs[pos:]
    h = _norm_mod(x_ref[0], g_ref[...], sc_ref[0], sh_ref[0]).astype(bf16)
    qkv = jnp.dot(h, w_ref[...], preferred_element_type=f32)

    def head(t, gain, scale):
        t = t * lax.rsqrt(jnp.mean(t * t, axis=-1, keepdims=True) + EPS) * gain
        if rope:
            t = t * cos_ref[...] + pltpu.roll(t, HEAD_DIM // 2, axis=1) * sin_ref[...]
        return (t * scale).astype(bf16)

    off = 0
    if with_q:
        q_ref, k_ref, v_ref = out_refs
        for hd in range(N_HEADS):
            cols = slice(hd * HEAD_DIM, (hd + 1) * HEAD_DIM)
            q_ref[0, :, cols] = head(qkv[:, cols], qg_ref[...], HEAD_DIM ** -0.5)
        off = N_HEADS * HEAD_DIM
    else:
        k_ref, v_ref = out_refs
    for kv in range(N_KV):
        cols = slice(kv * HEAD_DIM, (kv + 1) * HEAD_DIM)
        src = slice(off + kv * HEAD_DIM, off + (kv + 1) * HEAD_DIM)
        k_ref[0, :, cols] = head(qkv[:, src], kg_ref[...], 1.0)
    nk = N_KV * HEAD_DIM
    v_ref[0] = qkv[:, off + nk:off + 2 * nk].astype(bf16)


def _qkv(x, g, mods3, row_of, w, q_g, k_g, tables, with_q, tm):
    b, n, _ = x.shape
    nq = N_HEADS * HEAD_DIM
    nk = N_KV * HEAD_DIM
    rope = tables is not None
    nw = w.shape[1]
    in_specs = [
        pl.BlockSpec((1, tm, D), lambda b, i: (b, i, 0)),
        pl.BlockSpec((1, D), lambda b, i: (0, 0)),
        _mod_spec(row_of, 0, 2),
        _mod_spec(row_of, 1, 2),
        pl.BlockSpec((D, nw), lambda b, i: (0, 0)),
        pl.BlockSpec((1, HEAD_DIM), lambda b, i: (0, 0)),
        pl.BlockSpec((1, HEAD_DIM), lambda b, i: (0, 0)),
    ]
    args = [x, g, mods3, mods3, w, q_g, k_g]
    if rope:
        in_specs += [pl.BlockSpec((tm, HEAD_DIM), lambda b, i: (i, 0))] * 2
        args += list(tables)
    out_specs = [pl.BlockSpec((1, tm, nk), lambda b, i: (b, i, 0))] * 2
    out_shape = [jax.ShapeDtypeStruct((b, n, nk), bf16)] * 2
    if with_q:
        out_specs = [pl.BlockSpec((1, tm, nq), lambda b, i: (b, i, 0))] + out_specs
        out_shape = [jax.ShapeDtypeStruct((b, n, nq), bf16)] + out_shape
    return pl.pallas_call(
        functools.partial(_qkv_kernel, with_q=with_q, rope=rope),
        grid=(b, n // tm),
        in_specs=in_specs,
        out_specs=out_specs,
        out_shape=out_shape,
        compiler_params=_params(("arbitrary", "arbitrary")),
    )(*args)


def _attn_kernel(*refs, nseg):
    q_ref = refs[0]
    k_refs = refs[1:1 + nseg]
    v_refs = refs[1 + nseg:1 + 2 * nseg]
    o_ref = refs[1 + 2 * nseg]
    q = q_ref[0]
    nt = (((1,), (1,)), ((), ()))
    s = [lax.dot_general(q, k[0], nt, preferred_element_type=f32) for k in k_refs]
    m = jnp.max(s[0], axis=-1, keepdims=True)
    for t in s[1:]:
        m = jnp.maximum(m, jnp.max(t, axis=-1, keepdims=True))
    e = [jnp.exp(t - m) for t in s]
    l = jnp.sum(e[0], axis=-1, keepdims=True)
    o = jnp.dot(e[0].astype(bf16), v_refs[0][0], preferred_element_type=f32)
    for t, v in zip(e[1:], v_refs[1:]):
        l = l + jnp.sum(t, axis=-1, keepdims=True)
        o = o + jnp.dot(t.astype(bf16), v[0], preferred_element_type=f32)
    o_ref[0] = (o / l).astype(o_ref.dtype)


def _attention(q, ks, vs, tq):
    b, n, _ = q.shape
    nseg = len(ks)
    kv_specs = [pl.BlockSpec((1, k.shape[1], HEAD_DIM), lambda b, h, i: (b, 0, h // GQA)) for k in ks]
    return pl.pallas_call(
        functools.partial(_attn_kernel, nseg=nseg),
        grid=(b, N_HEADS, n // tq),
        in_specs=[pl.BlockSpec((1, tq, HEAD_DIM), lambda b, h, i: (b, i, h))] + kv_specs + kv_specs,
        out_specs=pl.BlockSpec((1, tq, HEAD_DIM), lambda b, h, i: (b, i, h)),
        out_shape=jax.ShapeDtypeStruct((b, n, N_HEADS * HEAD_DIM), bf16),
        compiler_params=_params(("arbitrary", "arbitrary", "arbitrary")),
    )(q, *ks, *vs)


def _final_kernel(x_ref, g_ref, o_ref):
    x = x_ref[0]
    o_ref[0] = x * lax.rsqrt(jnp.mean(x * x, axis=-1, keepdims=True) + EPS) * g_ref[...]


def _final_norm(x, g, tm):
    b, n, _ = x.shape
    return pl.pallas_call(
        _final_kernel,
        grid=(b, n // tm),
        in_specs=[pl.BlockSpec((1, tm, D), lambda b, i: (b, i, 0)),
                  pl.BlockSpec((1, D), lambda b, i: (0, 0))],
        out_specs=pl.BlockSpec((1, tm, D), lambda b, i: (b, i, 0)),
        out_shape=jax.ShapeDtypeStruct((b, n, D), f32),
        compiler_params=_params(("arbitrary", "arbitrary")),
    )(x, g)


def _pad2(a, rows, cols):
    return jnp.pad(a, ((0, rows - a.shape[0]), (0, cols - a.shape[1])))


def kernel(x, c, ctx, c_ctx, mod_w, mod_b, norm1_g, norm2_g, ffn_w_gu, ffn_w_down, even_w_in,
           gmlp_ln_g, gmlp_ln_b, gmlp_w_s, gmlp_b_s, hyena_conv_w, hyena_conv_b, hyena_f_w1,
           hyena_f_b1, hyena_f_w2, hyena_f_b2, hyena_f_w3, hyena_freq, hyena_skip, even_w_out,
           attn_w_qkv, attn_q_g, attn_k_g, attn_w_o, final_g):
    lat_row = lambda b: b
    ctx_row = lambda b: CTX_ROW
    streams = ((SEQ, lat_row), (CTX, ctx_row))

    cond = jnp.zeros((MOD_ROWS, D), f32).at[:NB].set(c).at[CTX_ROW].set(c_ctx)
    mods = _mods(cond, mod_w, mod_b)

    ffwd, finv, gmat = _dft_consts(HY_P)
    fconst = {n: _filter_consts(n) for n in (SEQ, CTX)}
    rope = _rope_tables(SEQ)

    xs = {SEQ: x, CTX: ctx}
    for layer in range(DEPTH):
        last = layer == DEPTH - 1
        is_even = layer % 2 == 0
        mods3 = mods[layer].reshape(MOD_ROWS, 1, 6 * D)
        g1 = norm1_g[layer].reshape(1, D)
        g2 = norm2_g[layer].reshape(1, D)
        w_gu = ffn_w_gu[layer].astype(bf16)
        w_down = ffn_w_down[layer].astype(bf16)
        active = [s for s in streams if s[0] == SEQ or not (last and is_even)]
        if is_even:
            i = layer // 2
            w_in = even_w_in[i].astype(bf16)
            w_out = even_w_out[i].astype(bf16)
            w_s = gmlp_w_s[i].astype(bf16)
            b_s_rows = jnp.repeat(gmlp_b_s[i].T, CHUNK, axis=1)
            ln_g = gmlp_ln_g[i].reshape(1, D)
            ln_b = gmlp_ln_b[i].reshape(1, D)
            conv_w3 = hyena_conv_w[i].reshape(3, 3, D).transpose(1, 0, 2)
            conv_b3 = hyena_conv_b[i].reshape(3, 1, D)
            w1p = _pad2(hyena_f_w1[i], 128, 128)
            w2p = _pad2(hyena_f_w2[i], 128, 128)
            b1p = _pad2(hyena_f_b1[i][None, :], 1, 128)
            b2p = _pad2(hyena_f_b2[i][None, :], 1, 128)
            f0p = _pad2(hyena_freq[i, 0][None, :], 1, 128)
            f1p = _pad2(hyena_freq[i, 1][None, :], 1, 128)
            w3p = jnp.pad(hyena_f_w3[i].reshape(FILTER_HIDDEN, 2, 2 * D).transpose(1, 0, 2),
                          ((0, 0), (0, 128 - FILTER_HIDDEN), (0, 0)))
            for n, row_of in active:
                xc = xs[n]
                zemb, deltas = fconst[n]
                hraw, hsum = _filter_time(n, zemb, deltas, w1p, b1p, w2p, b2p, f0p, f1p, w3p)
                h_a, h_b = _filter_spec(n, HY_P, gmat, hraw, hsum)
                tm = min(n, 512)
                proj = _norm_mod_matmul(xc, g1, mods3, row_of, w_in, tm, 1024)
                y_a = _gmlp(proj, ln_g, ln_b, w_s, b_s_rows, 256)
                z_b = _hyena(proj, conv_w3, conv_b3, hyena_skip[i], h_a, h_b, ffwd, finv, HY_P)
                xs[n] = _proj_residual(xc, mods3, row_of, 2, [y_a, z_b], [w_out[:D], w_out[D:]], tm)
        else:
            j = layer // 2
            w_qkv = attn_w_qkv[j].astype(bf16)
            w_o = attn_w_o[j].astype(bf16)
            q_g = attn_q_g[j].reshape(1, HEAD_DIM)
            k_g = attn_k_g[j].reshape(1, HEAD_DIM)
            nq = N_HEADS * HEAD_DIM
            q_l, k_l, v_l = _qkv(xs[SEQ], g1, mods3, lat_row, w_qkv, q_g, k_g, rope, True, 512)
            if last:
                k_c, v_c = _qkv(xs[CTX], g1, mods3, ctx_row, w_qkv[:, nq:], q_g, k_g, None, False, 256)
            else:
                q_c, k_c, v_c = _qkv(xs[CTX], g1, mods3, ctx_row, w_qkv, q_g, k_g, None, True, 256)
            o_l = _attention(q_l, [k_c, k_l], [v_c, v_l], 256)
            new = {SEQ: _proj_residual(xs[SEQ], mods3, lat_row, 2, [o_l], [w_o], 512)}
            if not last:
                o_c = _attention(q_c, [k_c], [v_c], 256)
                new[CTX] = _proj_residual(xs[CTX], mods3, ctx_row, 2, [o_c], [w_o], 256)
            xs.update(new)
        for n, row_of in streams:
            if n == CTX and last:
                continue
            xs[n] = _ffn(xs[n], g2, mods3, row_of, w_gu, w_down, min(n, 512), 256)
    return _final_norm(xs[SEQ], final_g.reshape(1, D), 512)
```

```python
import functools
import math

import numpy as np
import jax
import jax.numpy as jnp
from jax import lax
from jax.experimental import pallas as pl
from jax.experimental.pallas import tpu as pltpu

D = 1024
NB = 16
SEQ = 2048
CTX = 256
DEPTH = 4
GRID_W = 64
EPS = 1e-6
CHUNK = 128
A_GROUPS = 8
HEAD_DIM = 128
N_HEADS = 8
N_KV = 2
GQA = N_HEADS // N_KV
D_FF = 2816
FILTER_EMB = 33
FILTER_BANDS = 16
FILTER_HIDDEN = 64
MIN_DECAY = math.log(1e-2) / 1.5
MAX_DECAY = math.log(1e-2) / 0.3
ROPE_THETA = 10000.0

LANES = 128
CTX_ROW = NB
MOD_ROWS = 24
HY_P = {SEQ: 512, CTX: 256}
HY_TC = LANES
HY_NBAT = 2
HY_ROWS = 32
FFN_TF = 256
VMEM_LIMIT = 56 * 1024 * 1024

bf16 = jnp.bfloat16
f32 = jnp.float32


def _dot(a, b):
    return jnp.dot(a, b, preferred_element_type=f32)


def _mm_f32(a, b):
    return jnp.dot(a, b, preferred_element_type=f32, precision=lax.Precision.HIGHEST)


def _params(sem):
    return pltpu.CompilerParams(dimension_semantics=sem, vmem_limit_bytes=VMEM_LIMIT)


def _whole(a):
    nd = a.ndim
    return pl.BlockSpec(a.shape, lambda *_: (0,) * nd)


def _mods_kernel(cond_ref, w_ref, b_ref, o_ref):
    a = jax.nn.silu(cond_ref[...]).astype(bf16)
    o_ref[0] = _dot(a, w_ref[0].astype(bf16)) + b_ref[0]


def _mods(cond, mod_w, mod_b):
    tn = 1024
    n6 = 6 * D
    return pl.pallas_call(
        _mods_kernel,
        grid=(DEPTH, n6 // tn),
        in_specs=[
            pl.BlockSpec((MOD_ROWS, D), lambda l, j: (0, 0)),
            pl.BlockSpec((1, D, tn), lambda l, j: (l, 0, j)),
            pl.BlockSpec((1, 1, tn), lambda l, j: (l, 0, j)),
        ],
        out_specs=pl.BlockSpec((1, MOD_ROWS, tn), lambda l, j: (l, 0, j)),
        out_shape=jax.ShapeDtypeStruct((DEPTH, MOD_ROWS, n6), f32),
        compiler_params=_params(("arbitrary", "arbitrary")),
    )(cond, mod_w, mod_b.reshape(DEPTH, 1, n6))


def _mod_spec(row_of, chunk):
    return pl.BlockSpec((1, 1, D), lambda b, i: (row_of(b), 0, chunk))


def _row_spec(tm, width):
    return pl.BlockSpec((1, tm, width), lambda b, i: (b, i, 0))


def _norm_mod(x, g, sc, sh):
    y = x * lax.rsqrt(jnp.mean(x * x, axis=-1, keepdims=True) + EPS)
    return (y * g) * (1.0 + sc) + sh


def _gelu(x):
    return 0.5 * x * (1.0 + lax.erf(x * (2.0 ** -0.5)))


def _even_in_kernel(x_ref, g_ref, sh_ref, sc_ref, w_ref, lng_ref, lnb_ref, ws_ref, bs_ref,
                    ya_ref, pb_ref, *, tm):
    h = _norm_mod(x_ref[0], g_ref[...], sc_ref[0], sh_ref[0]).astype(bf16)
    pb_ref[0] = _dot(h, w_ref[:, 2 * D:])
    u = _gelu(_dot(h, w_ref[:, :D]))
    v = _gelu(_dot(h, w_ref[:, D:2 * D]))
    mu = jnp.mean(v, axis=-1, keepdims=True)
    vc = v - mu
    var = jnp.mean(vc * vc, axis=-1, keepdims=True)
    vn = (vc * lax.rsqrt(var + 1e-5) * lng_ref[...] + lnb_ref[...]).astype(bf16)
    nck = tm // CHUNK
    for g in range(A_GROUPS):
        cols = slice(g * CHUNK, (g + 1) * CHUNK)
        rhs = jnp.concatenate([vn[k * CHUNK:(k + 1) * CHUNK, cols] for k in range(nck)], axis=1)
        sv = _dot(ws_ref[g], rhs)
        for k in range(nck):
            rows = slice(k * CHUNK, (k + 1) * CHUNK)
            gate = sv[:, k * CHUNK:(k + 1) * CHUNK] + bs_ref[:, cols]
            ya_ref[0, rows, cols] = (u[rows, cols] * gate).astype(bf16)


def _even_in(x, g, mods3, row_of, w_in, ln_g, ln_b, w_s, b_s_rows, tm):
    b, n, _ = x.shape
    return pl.pallas_call(
        functools.partial(_even_in_kernel, tm=tm),
        grid=(b, n // tm),
        in_specs=[
            _row_spec(tm, D), _whole(g), _mod_spec(row_of, 0), _mod_spec(row_of, 1),
            _whole(w_in), _whole(ln_g), _whole(ln_b), _whole(w_s), _whole(b_s_rows),
        ],
        out_specs=[_row_spec(tm, D), _row_spec(tm, 3 * D)],
        out_shape=[jax.ShapeDtypeStruct((b, n, D), bf16), jax.ShapeDtypeStruct((b, n, 3 * D), f32)],
        compiler_params=_params(("arbitrary", "arbitrary")),
    )(x, g, mods3, mods3, w_in, ln_g, ln_b, w_s, b_s_rows)


def _filter_consts(n):
    lag = np.abs(np.arange(2 * n) - n)
    lag[0] = 0
    t = np.linspace(0.0, 1.0, n)[lag][:, None]
    w = (2.0 * math.pi / n) * lag[:, None]
    bands = np.linspace(1e-4, FILTER_BANDS - 1, FILTER_BANDS)[None, :]
    ang = bands * w
    z = np.concatenate([t, np.cos(ang), -np.sin(ang)], axis=-1)
    zp = np.zeros((2 * n, LANES), np.float64)
    zp[:, :FILTER_EMB] = z
    deltas = np.abs(np.linspace(MIN_DECAY, MAX_DECAY, D))
    return jnp.asarray(zp, f32), jnp.asarray(np.tile(deltas, 2)[None, :], f32)


def _filter_time_kernel(z_ref, w1_ref, b1_ref, w2_ref, b2_ref, f0_ref, f1_ref, w3_ref, dl_ref,
                        h_ref, s_ref, *, tr):
    r = pl.program_id(0)
    z = z_ref[...]
    hid = jnp.sin(f0_ref[...] * (_mm_f32(z, w1_ref[...]) + b1_ref[...]))
    hid = jnp.sin(f1_ref[...] * (_mm_f32(hid, w2_ref[...]) + b2_ref[...]))
    h = _mm_f32(hid, w3_ref[0])
    h = h * jnp.exp(-z[:, 0:1] * dl_ref[...])
    row = lax.broadcasted_iota(jnp.int32, h.shape, 0) + r * tr
    h = jnp.where(row == 0, 0.0, h)
    h_ref[...] = h

    @pl.when(r == 0)
    def _():
        s_ref[...] = jnp.zeros_like(s_ref)

    s_ref[...] += jnp.sum(jnp.abs(h), axis=0, keepdims=True)


def _filter_time(n, zemb, deltas, w1p, b1p, w2p, b2p, f0p, f1p, w3p):
    tr = 256
    half = n // tr
    c2 = 2 * D
    return pl.pallas_call(
        functools.partial(_filter_time_kernel, tr=tr),
        grid=(2 * n // tr,),
        in_specs=[
            pl.BlockSpec((tr, LANES), lambda r: (r, 0)),
            _whole(w1p), _whole(b1p), _whole(w2p), _whole(b2p), _whole(f0p), _whole(f1p),
            pl.BlockSpec((1, LANES, c2), lambda r: (jnp.where(r < half, 1, 0), 0, 0)),
            _whole(deltas),
        ],
        out_specs=[pl.BlockSpec((tr, c2), lambda r: (r, 0)), pl.BlockSpec((1, c2), lambda r: (0, 0))],
        out_shape=[jax.ShapeDtypeStruct((2 * n, c2), f32), jax.ShapeDtypeStruct((1, c2), f32)],
        compiler_params=_params(("arbitrary",)),
    )(zemb, w1p, b1p, w2p, b2p, f0p, f1p, w3p, deltas)


def _split_bf16(a):
    hi = a.astype(bf16)
    return hi, (a - hi.astype(f32)).astype(bf16)


def _dft_consts(p):
    k = np.arange(p)
    theta = np.pi * (2 * k + 1) / (2 * p)
    s = np.arange(p)
    c = np.cos(theta[:, None] * s[None, :])
    sn = np.sin(theta[:, None] * s[None, :])
    ffwd = np.concatenate([c, -sn], axis=0)
    finv = np.concatenate([c.T, -sn.T], axis=1) / p
    m = np.arange(-p, p)
    g = np.concatenate([np.cos(theta[:, None] * m[None, :]),
                        -np.sin(theta[:, None] * m[None, :])], axis=0)
    g[:, 0] = 0.0
    g_hi, g_lo = _split_bf16(jnp.asarray(g, f32))
    return jnp.asarray(ffwd, bf16), jnp.asarray(finv, bf16), g_hi, g_lo


def _filter_spec_kernel(ghi_ref, glo_ref, lo_ref, hi_ref, s_ref, ha_ref, hb_ref, *, p):
    def mm3(cols, x):
        x_hi, x_lo = _split_bf16(x)
        g_hi = ghi_ref[:, cols]
        return _dot(g_hi, x_hi) + _dot(g_hi, x_lo) + _dot(glo_ref[:, cols], x_hi)

    h = mm3(slice(0, p), lo_ref[...]) + mm3(slice(p, 2 * p), hi_ref[...])
    h = h / s_ref[...]
    ha_ref[0] = h[:p]
    hb_ref[0] = h[p:]


def _filter_spec(n, p, g_hi, g_lo, hraw, hsum):
    nb = n // p
    c2 = 2 * D
    tc = 512
    nd = 2 * nb - 1
    out = jax.ShapeDtypeStruct((nd, p, c2), f32)
    return pl.pallas_call(
        functools.partial(_filter_spec_kernel, p=p),
        grid=(nd, c2 // tc),
        in_specs=[
            _whole(g_hi), _whole(g_lo),
            pl.BlockSpec((p, tc), lambda d, c: (d, c)),
            pl.BlockSpec((p, tc), lambda d, c: (d + 1, c)),
            pl.BlockSpec((1, tc), lambda d, c: (0, c)),
        ],
        out_specs=[pl.BlockSpec((1, p, tc), lambda d, c: (d, 0, c))] * 2,
        out_shape=[out, out],
        compiler_params=_params(("arbitrary", "arbitrary")),
    )(g_hi, g_lo, hraw, hraw, hsum)


def _hyena_kernel(pv_ref, p1_ref, p2_ref, cw_ref, cb_ref, skip_ref,
                  ha0_ref, hb0_ref, ha1_ref, hb1_ref, ffwd_ref, finv_ref,
                  o_ref, sv_ref, s1_ref, s2_ref, u_ref, za_ref, zb_ref, yc_ref, *, n, p):
    nb = n // p
    tc = HY_TC
    halves = [slice(bb * tc, (bb + 1) * tc) for bb in range(HY_NBAT)]

    def short_conv(p_ref, part, dst_ref):
        w = cw_ref[part]
        w0, w1, w2, bias = w[0:1], w[1:2], w[2:3], cb_ref[part]
        for bb, lanes in enumerate(halves):
            x = p_ref[bb]
            dst_ref[:, lanes] = (pltpu.roll(x, 1, axis=0) * w0 + x * w1
                                 + pltpu.roll(x, n - 1, axis=0) * w2 + bias)
            dst_ref[0:1, lanes] = x[0:1] * w1 + x[1:2] * w2 + bias
            dst_ref[n - 1:n, lanes] = x[n - 2:n - 1] * w0 + x[n - 1:n] * w1 + bias

    short_conv(pv_ref, 0, sv_ref)
    short_conv(p1_ref, 1, s1_ref)
    short_conv(p2_ref, 2, s2_ref)

    def long_conv(src_ref, gate_ref, ha_ref, hb_ref, skip, store):
        for j in range(nb):
            zt = _dot(ffwd_ref[...], src_ref[j * p:(j + 1) * p, :].astype(bf16))
            za_ref[j] = zt[:p]
            zb_ref[j] = zt[p:]
        for i in range(nb):
            for c in range(p // HY_ROWS):
                rows = slice(c * HY_ROWS, (c + 1) * HY_ROWS)
                ya = [jnp.zeros((HY_ROWS, tc), f32) for _ in halves]
                yb = [jnp.zeros((HY_ROWS, tc), f32) for _ in halves]
                for j in range(nb):
                    d = i - j + (nb - 1)
                    h_a = ha_ref[d, rows, :]
                    h_b = hb_ref[d, rows, :]
                    for bb, lanes in enumerate(halves):
                        z_a = za_ref[j, rows, lanes]
                        z_b = zb_ref[j, rows, lanes]
                        ya[bb] = ya[bb] + h_a * z_a - h_b * z_b
                        yb[bb] = yb[bb] + h_a * z_b + h_b * z_a
                for bb, lanes in enumerate(halves):
                    yc_ref[i, c * HY_ROWS:(c + 1) * HY_ROWS, lanes] = ya[bb].astype(bf16)
                    yc_ref[i, p + c * HY_ROWS:p + (c + 1) * HY_ROWS, lanes] = yb[bb].astype(bf16)
            y = _dot(finv_ref[...], yc_ref[i])
            blk = slice(i * p, (i + 1) * p)
            store(blk, gate_ref[blk, :] * (y + src_ref[blk, :] * skip))

    def store_u(blk, val):
        u_ref[blk, :] = val

    def store_o(blk, val):
        for bb, lanes in enumerate(halves):
            o_ref[bb, blk, :] = val[:, lanes].astype(o_ref.dtype)

    skip0 = jnp.concatenate([skip_ref[0:1, :]] * HY_NBAT, axis=1)
    skip1 = jnp.concatenate([skip_ref[1:2, :]] * HY_NBAT, axis=1)
    long_conv(sv_ref, s1_ref, ha0_ref, hb0_ref, skip0, store_u)
    long_conv(u_ref, s2_ref, ha1_ref, hb1_ref, skip1, store_o)


def _hyena(pb, conv_w3, conv_b3, skip, h_a, h_b, ffwd, finv, p):
    b, n, _ = pb.shape
    tc = HY_TC
    nct = D // tc
    nb = n // p
    nd = 2 * nb - 1
    wide = HY_NBAT * tc

    def pspec(part):
        return pl.BlockSpec((HY_NBAT, n, tc), lambda c, b: (b, 0, part * nct + c))

    def hspec(order):
        return pl.BlockSpec((nd, p, tc), lambda c, b: (0, 0, order * nct + c),
                            pipeline_mode=pl.Buffered(1))

    return pl.pallas_call(
        functools.partial(_hyena_kernel, n=n, p=p),
        grid=(nct, b // HY_NBAT),
        in_specs=[
            pspec(0), pspec(1), pspec(2),
            pl.BlockSpec((3, 3, tc), lambda c, b: (0, 0, c)),
            pl.BlockSpec((3, 1, tc), lambda c, b: (0, 0, c)),
            pl.BlockSpec((2, tc), lambda c, b: (0, c)),
            hspec(0), hspec(0), hspec(1), hspec(1),
            _whole(ffwd), _whole(finv),
        ],
        out_specs=pl.BlockSpec((HY_NBAT, n, tc), lambda c, b: (b, 0, c)),
        out_shape=jax.ShapeDtypeStruct((b, n, D), bf16),
        scratch_shapes=[
            pltpu.VMEM((n, wide), f32), pltpu.VMEM((n, wide), f32), pltpu.VMEM((n, wide), f32),
            pltpu.VMEM((n, wide), f32),
            pltpu.VMEM((nb, p, wide), f32), pltpu.VMEM((nb, p, wide), f32),
            pltpu.VMEM((nb, 2 * p, wide), bf16),
        ],
        compiler_params=_params(("arbitrary", "arbitrary")),
    )(pb, pb, pb, conv_w3, conv_b3, skip, h_a, h_b, h_a, h_b, ffwd, finv)


def _proj_res_kernel(*refs, na):
    x_ref, gate_ref = refs[0], refs[1]
    a_refs = refs[2:2 + na]
    w_refs = refs[2 + na:2 + 2 * na]
    o_ref = refs[2 + 2 * na]
    acc = _dot(a_refs[0][0], w_refs[0][...])
    for k in range(1, na):
        acc = acc + _dot(a_refs[k][0], w_refs[k][...])
    o_ref[0] = x_ref[0] + gate_ref[0] * acc


def _proj_residual(x, mods3, row_of, gate_chunk, acts, ws, tm):
    b, n, _ = x.shape
    na = len(acts)
    return pl.pallas_call(
        functools.partial(_proj_res_kernel, na=na),
        grid=(b, n // tm),
        in_specs=([_row_spec(tm, D), _mod_spec(row_of, gate_chunk)]
                  + [_row_spec(tm, a.shape[-1]) for a in acts] + [_whole(w) for w in ws]),
        out_specs=_row_spec(tm, D),
        out_shape=jax.ShapeDtypeStruct((b, n, D), f32),
        compiler_params=_params(("arbitrary", "arbitrary")),
    )(x, mods3, *acts, *ws)


def _ffn_kernel(x_ref, g_ref, sh_ref, sc_ref, gate_ref, wgu_ref, wd_ref, o_ref):
    h = _norm_mod(x_ref[0], g_ref[...], sc_ref[0], sh_ref[0]).astype(bf16)
    acc = None
    for f in range(D_FF // FFN_TF):
        cols = slice(f * FFN_TF, (f + 1) * FFN_TF)
        ucols = slice(D_FF + f * FFN_TF, D_FF + (f + 1) * FFN_TF)
        act = (jax.nn.silu(_dot(h, wgu_ref[:, cols])) * _dot(h, wgu_ref[:, ucols])).astype(bf16)
        part = _dot(act, wd_ref[cols, :])
        acc = part if acc is None else acc + part
    o_ref[0] = x_ref[0] + gate_ref[0] * acc


def _ffn(x, g, mods3, row_of, w_gu, w_down, tm):
    b, n, _ = x.shape
    return pl.pallas_call(
        _ffn_kernel,
        grid=(b, n // tm),
        in_specs=[_row_spec(tm, D), _whole(g), _mod_spec(row_of, 3), _mod_spec(row_of, 4),
                  _mod_spec(row_of, 5), _whole(w_gu), _whole(w_down)],
        out_specs=_row_spec(tm, D),
        out_shape=jax.ShapeDtypeStruct((b, n, D), f32),
        compiler_params=_params(("arbitrary", "arbitrary")),
    )(x, g, mods3, mods3, mods3, w_gu, w_down)


def _rope_tables(n):
    rows = n // GRID_W
    row = np.repeat(np.arange(rows), GRID_W).astype(np.float64)
    col = np.tile(np.arange(GRID_W), rows).astype(np.float64)
    half = HEAD_DIM // 2
    inv = ROPE_THETA ** (-np.arange(0, half, 2, dtype=np.float64) / half)
    ang = np.concatenate([row[:, None] * inv, col[:, None] * inv], axis=-1)
    cos, sin = np.cos(ang), np.sin(ang)
    return (jnp.asarray(np.concatenate([cos, cos], axis=-1), f32),
            jnp.asarray(np.concatenate([-sin, sin], axis=-1), f32))


def _qkv_kernel(*refs, with_q, rope):
    x_ref, g_ref, sh_ref, sc_ref, w_ref, qg_ref, kg_ref = refs[:7]
    pos = 7
    if rope:
        cos_ref, sin_ref = refs[7:9]
        pos = 9
    out_refs = refs[pos:]
    h = _norm_mod(x_ref[0], g_ref[...], sc_ref[0], sh_ref[0]).astype(bf16)
    qkv = _dot(h, w_ref[...])

    def head(t, gain, scale):
        t = t * lax.rsqrt(jnp.mean(t * t, axis=-1, keepdims=True) + EPS) * gain
        if rope:
            t = t * cos_ref[...] + pltpu.roll(t, HEAD_DIM // 2, axis=1) * sin_ref[...]
        return (t * scale).astype(bf16)

    off = 0
    if with_q:
        q_ref, k_ref, v_ref = out_refs
        for hd in range(N_HEADS):
            cols = slice(hd * HEAD_DIM, (hd + 1) * HEAD_DIM)
            q_ref[0, :, cols] = head(qkv[:, cols], qg_ref[...], HEAD_DIM ** -0.5)
        off = N_HEADS * HEAD_DIM
    else:
        k_ref, v_ref = out_refs
    for kv in range(N_KV):
        cols = slice(kv * HEAD_DIM, (kv + 1) * HEAD_DIM)
        src = slice(off + kv * HEAD_DIM, off + (kv + 1) * HEAD_DIM)
        k_ref[0, :, cols] = head(qkv[:, src], kg_ref[...], 1.0)
    nk = N_KV * HEAD_DIM
    v_ref[0] = qkv[:, off + nk:off + 2 * nk].astype(bf16)


def _qkv(x, g, mods3, row_of, w, q_g, k_g, tables, with_q, tm):
    b, n, _ = x.shape
    nq = N_HEADS * HEAD_DIM
    nk = N_KV * HEAD_DIM
    rope = tables is not None
    in_specs = [_row_spec(tm, D), _whole(g), _mod_spec(row_of, 0), _mod_spec(row_of, 1),
                _whole(w), _whole(q_g), _whole(k_g)]
    args = [x, g, mods3, mods3, w, q_g, k_g]
    if rope:
        in_specs += [pl.BlockSpec((tm, HEAD_DIM), lambda b, i: (i, 0))] * 2
        args += list(tables)
    out_specs = [_row_spec(tm, nk)] * 2
    out_shape = [jax.ShapeDtypeStruct((b, n, nk), bf16)] * 2
    if with_q:
        out_specs = [_row_spec(tm, nq)] + out_specs
        out_shape = [jax.ShapeDtypeStruct((b, n, nq), bf16)] + out_shape
    return pl.pallas_call(
        functools.partial(_qkv_kernel, with_q=with_q, rope=rope),
        grid=(b, n // tm),
        in_specs=in_specs,
        out_specs=out_specs,
        out_shape=out_shape,
        compiler_params=_params(("arbitrary", "arbitrary")),
    )(*args)


def _attn_kernel(*refs, nseg):
    q_ref = refs[0]
    k_refs = refs[1:1 + nseg]
    v_refs = refs[1 + nseg:1 + 2 * nseg]
    o_ref = refs[1 + 2 * nseg]
    nt = (((1,), (1,)), ((), ()))
    for hd in range(GQA):
        cols = slice(hd * HEAD_DIM, (hd + 1) * HEAD_DIM)
        q = q_ref[0, :, cols]
        s = [lax.dot_general(q, k[0], nt, preferred_element_type=f32) for k in k_refs]
        m = jnp.max(s[0], axis=-1, keepdims=True)
        for t in s[1:]:
            m = jnp.maximum(m, jnp.max(t, axis=-1, keepdims=True))
        e = [jnp.exp(t - m) for t in s]
        l = jnp.sum(e[0], axis=-1, keepdims=True)
        o = _dot(e[0].astype(bf16), v_refs[0][0])
        for t, v in zip(e[1:], v_refs[1:]):
            l = l + jnp.sum(t, axis=-1, keepdims=True)
            o = o + _dot(t.astype(bf16), v[0])
        o_ref[0, :, cols] = (o / l).astype(o_ref.dtype)


def _attention(q, ks, vs, tq):
    b, n, _ = q.shape
    nseg = len(ks)
    gw = GQA * HEAD_DIM
    kv_specs = [pl.BlockSpec((1, k.shape[1], HEAD_DIM), lambda b, h, i: (b, 0, h)) for k in ks]
    return pl.pallas_call(
        functools.partial(_attn_kernel, nseg=nseg),
        grid=(b, N_KV, n // tq),
        in_specs=[pl.BlockSpec((1, tq, gw), lambda b, h, i: (b, i, h))] + kv_specs + kv_specs,
        out_specs=pl.BlockSpec((1, tq, gw), lambda b, h, i: (b, i, h)),
        out_shape=jax.ShapeDtypeStruct((b, n, N_HEADS * HEAD_DIM), bf16),
        compiler_params=_params(("arbitrary", "arbitrary", "arbitrary")),
    )(q, *ks, *vs)


def _final_kernel(x_ref, g_ref, o_ref):
    x = x_ref[0]
    o_ref[0] = x * lax.rsqrt(jnp.mean(x * x, axis=-1, keepdims=True) + EPS) * g_ref[...]


def _final_norm(x, g, tm):
    b, n, _ = x.shape
    return pl.pallas_call(
        _final_kernel,
        grid=(b, n // tm),
        in_specs=[_row_spec(tm, D), _whole(g)],
        out_specs=_row_spec(tm, D),
        out_shape=jax.ShapeDtypeStruct((b, n, D), f32),
        compiler_params=_params(("arbitrary", "arbitrary")),
    )(x, g)


def _pad2(a, rows, cols):
    return jnp.pad(a, ((0, rows - a.shape[0]), (0, cols - a.shape[1])))


def kernel(x, c, ctx, c_ctx, mod_w, mod_b, norm1_g, norm2_g, ffn_w_gu, ffn_w_down, even_w_in,
           gmlp_ln_g, gmlp_ln_b, gmlp_w_s, gmlp_b_s, hyena_conv_w, hyena_conv_b, hyena_f_w1,
           hyena_f_b1, hyena_f_w2, hyena_f_b2, hyena_f_w3, hyena_freq, hyena_skip, even_w_out,
           attn_w_qkv, attn_q_g, attn_k_g, attn_w_o, final_g):
    lat_row = lambda b: b
    ctx_row = lambda b: CTX_ROW
    streams = {"lat": (SEQ, lat_row, 512), "ctx": (CTX, ctx_row, 512)}

    cond = jnp.zeros((MOD_ROWS, D), f32).at[:NB].set(c).at[CTX_ROW].set(c_ctx)
    mods = _mods(cond, mod_w, mod_b)

    dft = {p: _dft_consts(p) for p in set(HY_P.values())}
    fconst = {n: _filter_consts(n) for n in (SEQ, CTX)}
    rope = _rope_tables(SEQ)
    nq = N_HEADS * HEAD_DIM
    nk = N_KV * HEAD_DIM

    xs = {"lat": x, "ctx": ctx.reshape(1, NB * CTX, D)}
    for layer in range(DEPTH):
        last = layer == DEPTH - 1
        is_even = layer % 2 == 0
        mods3 = mods[layer].reshape(MOD_ROWS, 1, 6 * D)
        g1 = norm1_g[layer].reshape(1, D)
        g2 = norm2_g[layer].reshape(1, D)
        w_gu = ffn_w_gu[layer].astype(bf16)
        w_down = ffn_w_down[layer].astype(bf16)
        if is_even:
            i = layer // 2
            w_in = even_w_in[i].astype(bf16)
            w_out = even_w_out[i].astype(bf16)
            w_s = gmlp_w_s[i].astype(bf16)
            b_s_rows = jnp.repeat(gmlp_b_s[i].T, CHUNK, axis=1)
            ln_g = gmlp_ln_g[i].reshape(1, D)
            ln_b = gmlp_ln_b[i].reshape(1, D)
            conv_w3 = hyena_conv_w[i].reshape(3, 3, D).transpose(1, 0, 2)
            conv_b3 = hyena_conv_b[i].reshape(3, 1, D)
            w1p = _pad2(hyena_f_w1[i], LANES, LANES)
            w2p = _pad2(hyena_f_w2[i], LANES, LANES)
            b1p = _pad2(hyena_f_b1[i][None, :], 1, LANES)
            b2p = _pad2(hyena_f_b2[i][None, :], 1, LANES)
            f0p = _pad2(hyena_freq[i, 0][None, :], 1, LANES)
            f1p = _pad2(hyena_freq[i, 1][None, :], 1, LANES)
            w3p = jnp.pad(hyena_f_w3[i].reshape(FILTER_HIDDEN, 2, 2 * D).transpose(1, 0, 2),
                          ((0, 0), (0, LANES - FILTER_HIDDEN), (0, 0)))
            for key, (n, row_of, tm) in streams.items():
                if key == "ctx" and last:
                    continue
                xc = xs[key]
                p = HY_P[n]
                ffwd, finv, g_hi, g_lo = dft[p]
                zemb, deltas = fconst[n]
                hraw, hsum = _filter_time(n, zemb, deltas, w1p, b1p, w2p, b2p, f0p, f1p, w3p)
                h_a, h_b = _filter_spec(n, p, g_hi, g_lo, hraw, hsum)
                y_a, pb = _even_in(xc, g1, mods3, row_of, w_in, ln_g, ln_b, w_s, b_s_rows, tm)
                z_b = _hyena(pb.reshape(NB, n, 3 * D), conv_w3, conv_b3, hyena_skip[i],
                             h_a, h_b, ffwd, finv, p)
                xs[key] = _proj_residual(xc, mods3, row_of, 2, [y_a, z_b.reshape(y_a.shape)],
                                         [w_out[:D], w_out[D:]], tm)
        else:
            j = layer // 2
            w_qkv = attn_w_qkv[j].astype(bf16)
            w_o = attn_w_o[j].astype(bf16)
            q_g = attn_q_g[j].reshape(1, HEAD_DIM)
            k_g = attn_k_g[j].reshape(1, HEAD_DIM)
            q_l, k_l, v_l = _qkv(xs["lat"], g1, mods3, lat_row, w_qkv, q_g, k_g, rope, True, 512)
            if last:
                k_c, v_c = _qkv(xs["ctx"], g1, mods3, ctx_row, w_qkv[:, nq:], q_g, k_g, None, False, 512)
            else:
                q_c, k_c, v_c = _qkv(xs["ctx"], g1, mods3, ctx_row, w_qkv, q_g, k_g, None, True, 512)
            k_c = k_c.reshape(NB, CTX, nk)
            v_c = v_c.reshape(NB, CTX, nk)
            o_l = _attention(q_l, [k_c, k_l], [v_c, v_l], 256)
            new = {"lat": _proj_residual(xs["lat"], mods3, lat_row, 2, [o_l], [w_o], 512)}
            if not last:
                o_c = _attention(q_c.reshape(NB, CTX, nq), [k_c], [v_c], CTX)
                new["ctx"] = _proj_residual(xs["ctx"], mods3, ctx_row, 2,
                                            [o_c.reshape(1, NB * CTX, nq)], [w_o], 512)
            xs.update(new)
        for key, (n, row_of, tm) in streams.items():
            if key == "ctx" and last:
                continue
            xs[key] = _ffn(xs[key], g2, mods3, row_of, w_gu, w_down, tm)
    return _final_norm(xs["lat"], final_g.reshape(1, D), 512)
```

```python
import functools
import math

import numpy as np
import jax
import jax.numpy as jnp
from jax import lax
from jax.experimental import pallas as pl
from jax.experimental.pallas import tpu as pltpu

D = 1024
NB = 16
SEQ = 2048
CTX = 256
DEPTH = 4
GRID_W = 64
EPS = 1e-6
CHUNK = 128
A_GROUPS = 8
HEAD_DIM = 128
N_HEADS = 8
N_KV = 2
GQA = N_HEADS // N_KV
D_FF = 2816
FILTER_EMB = 33
FILTER_BANDS = 16
FILTER_HIDDEN = 64
MIN_DECAY = math.log(1e-2) / 1.5
MAX_DECAY = math.log(1e-2) / 0.3
ROPE_THETA = 10000.0

LANES = 128
CTX_ROW = NB
MOD_ROWS = 24
HY_P = {SEQ: 512, CTX: 256}
HY_TC = LANES
HY_NBAT = 2
HY_ROWS = 32
FFN_TF = 256
VMEM_LIMIT = 56 * 1024 * 1024

bf16 = jnp.bfloat16
f32 = jnp.float32


_NT = (((1,), (1,)), ((), ()))


def _dot(a, b):
    return jnp.dot(a, b, preferred_element_type=f32)


def _mm_f32(a, b):
    return jnp.dot(a, b, preferred_element_type=f32, precision=lax.Precision.HIGHEST)


def _params(sem):
    return pltpu.CompilerParams(dimension_semantics=sem, vmem_limit_bytes=VMEM_LIMIT)


def _whole(a):
    nd = a.ndim
    return pl.BlockSpec(a.shape, lambda *_: (0,) * nd)


def _mods_kernel(cond_ref, w_ref, b_ref, o_ref):
    a = jax.nn.silu(cond_ref[...]).astype(bf16)
    o_ref[0] = _dot(a, w_ref[0].astype(bf16)) + b_ref[0]


def _mods(cond, mod_w, mod_b):
    tn = 1024
    n6 = 6 * D
    return pl.pallas_call(
        _mods_kernel,
        grid=(DEPTH, n6 // tn),
        in_specs=[
            pl.BlockSpec((MOD_ROWS, D), lambda l, j: (0, 0)),
            pl.BlockSpec((1, D, tn), lambda l, j: (l, 0, j)),
            pl.BlockSpec((1, 1, tn), lambda l, j: (l, 0, j)),
        ],
        out_specs=pl.BlockSpec((1, MOD_ROWS, tn), lambda l, j: (l, 0, j)),
        out_shape=jax.ShapeDtypeStruct((DEPTH, MOD_ROWS, n6), f32),
        compiler_params=_params(("arbitrary", "arbitrary")),
        name="mods",
    )(cond, mod_w, mod_b.reshape(DEPTH, 1, n6))


def _mod_spec(row_of, chunk):
    return pl.BlockSpec((1, 1, D), lambda b, i: (row_of(b), 0, chunk))


def _row_spec(tm, width):
    return pl.BlockSpec((1, tm, width), lambda b, i: (b, i, 0))


def _norm_mod(x, g, sc, sh):
    y = x * lax.rsqrt(jnp.mean(x * x, axis=-1, keepdims=True) + EPS)
    return (y * g) * (1.0 + sc) + sh


def _gelu(x):
    return 0.5 * x * (1.0 + lax.erf(x * (2.0 ** -0.5)))


def _even_in_kernel(x_ref, g_ref, sh_ref, sc_ref, w_ref, lng_ref, lnb_ref, ws_ref, bs_ref,
                    ya_ref, pb_ref, *, tm):
    h = _norm_mod(x_ref[0], g_ref[...], sc_ref[0], sh_ref[0]).astype(bf16)
    pb_ref[0] = _dot(h, w_ref[:, 2 * D:])
    u = _gelu(_dot(h, w_ref[:, :D]))
    v = _gelu(_dot(h, w_ref[:, D:2 * D]))
    mu = jnp.mean(v, axis=-1, keepdims=True)
    vc = v - mu
    var = jnp.mean(vc * vc, axis=-1, keepdims=True)
    vn = (vc * lax.rsqrt(var + 1e-5) * lng_ref[...] + lnb_ref[...]).astype(bf16)
    nck = tm // CHUNK
    for g in range(A_GROUPS):
        cols = slice(g * CHUNK, (g + 1) * CHUNK)
        rhs = jnp.concatenate([vn[k * CHUNK:(k + 1) * CHUNK, cols] for k in range(nck)], axis=1)
        sv = _dot(ws_ref[g], rhs)
        for k in range(nck):
            rows = slice(k * CHUNK, (k + 1) * CHUNK)
            gate = sv[:, k * CHUNK:(k + 1) * CHUNK] + bs_ref[:, cols]
            ya_ref[0, rows, cols] = (u[rows, cols] * gate).astype(bf16)


def _even_in(x, g, mods3, row_of, w_in, ln_g, ln_b, w_s, b_s_rows, tm):
    b, n, _ = x.shape
    return pl.pallas_call(
        functools.partial(_even_in_kernel, tm=tm),
        grid=(b, n // tm),
        in_specs=[
            _row_spec(tm, D), _whole(g), _mod_spec(row_of, 0), _mod_spec(row_of, 1),
            _whole(w_in), _whole(ln_g), _whole(ln_b), _whole(w_s), _whole(b_s_rows),
        ],
        out_specs=[_row_spec(tm, D), _row_spec(tm, 3 * D)],
        out_shape=[jax.ShapeDtypeStruct((b, n, D), bf16), jax.ShapeDtypeStruct((b, n, 3 * D), f32)],
        compiler_params=_params(("arbitrary", "arbitrary")),
        name="even_in",
    )(x, g, mods3, mods3, w_in, ln_g, ln_b, w_s, b_s_rows)


def _filter_consts(n):
    lag = np.abs(np.arange(2 * n) - n)
    lag[0] = 0
    t = np.linspace(0.0, 1.0, n)[lag][:, None]
    w = (2.0 * math.pi / n) * lag[:, None]
    bands = np.linspace(1e-4, FILTER_BANDS - 1, FILTER_BANDS)[None, :]
    ang = bands * w
    z = np.concatenate([t, np.cos(ang), -np.sin(ang)], axis=-1)
    zp = np.zeros((2 * n, LANES), np.float64)
    zp[:, :FILTER_EMB] = z
    deltas = np.abs(np.linspace(MIN_DECAY, MAX_DECAY, D))
    return jnp.asarray(zp, f32), jnp.asarray(np.tile(deltas, 2)[None, :], f32)


def _filter_time_kernel(z_ref, w1_ref, b1_ref, w2_ref, b2_ref, f0_ref, f1_ref, w3_ref, dl_ref,
                        h_ref, s_ref, *, tr):
    r = pl.program_id(0)
    z = z_ref[...]
    hid = jnp.sin(f0_ref[...] * (_mm_f32(z, w1_ref[...]) + b1_ref[...]))
    hid = jnp.sin(f1_ref[...] * (_mm_f32(hid, w2_ref[...]) + b2_ref[...]))
    h = _mm_f32(hid, w3_ref[0])
    h = h * jnp.exp(-z[:, 0:1] * dl_ref[...])
    row = lax.broadcasted_iota(jnp.int32, h.shape, 0) + r * tr
    h = jnp.where(row == 0, 0.0, h)
    h_ref[...] = h

    @pl.when(r == 0)
    def _():
        s_ref[...] = jnp.zeros_like(s_ref)

    s_ref[...] += jnp.sum(jnp.abs(h), axis=0, keepdims=True)


def _filter_time(n, zemb, deltas, w1p, b1p, w2p, b2p, f0p, f1p, w3p):
    tr = 256
    half = n // tr
    c2 = 2 * D
    return pl.pallas_call(
        functools.partial(_filter_time_kernel, tr=tr),
        grid=(2 * n // tr,),
        in_specs=[
            pl.BlockSpec((tr, LANES), lambda r: (r, 0)),
            _whole(w1p), _whole(b1p), _whole(w2p), _whole(b2p), _whole(f0p), _whole(f1p),
            pl.BlockSpec((1, LANES, c2), lambda r: (jnp.where(r < half, 1, 0), 0, 0)),
            _whole(deltas),
        ],
        out_specs=[pl.BlockSpec((tr, c2), lambda r: (r, 0)), pl.BlockSpec((1, c2), lambda r: (0, 0))],
        out_shape=[jax.ShapeDtypeStruct((2 * n, c2), f32), jax.ShapeDtypeStruct((1, c2), f32)],
        compiler_params=_params(("arbitrary",)),
        name="filter_time",
    )(zemb, w1p, b1p, w2p, b2p, f0p, f1p, w3p, deltas)


def _split_bf16(a):
    hi = a.astype(bf16)
    return hi, (a - hi.astype(f32)).astype(bf16)


def _dft_consts(p):
    k = np.arange(p)
    theta = np.pi * (2 * k + 1) / (2 * p)
    s = np.arange(p)
    c = np.cos(theta[:, None] * s[None, :])
    sn = np.sin(theta[:, None] * s[None, :])
    ffwd = np.concatenate([c, -sn], axis=0)
    finv = np.concatenate([c.T, -sn.T], axis=1) / p
    m = np.arange(-p, p)
    g = np.concatenate([np.cos(theta[:, None] * m[None, :]),
                        -np.sin(theta[:, None] * m[None, :])], axis=0)
    g[:, 0] = 0.0
    g_hi, g_lo = _split_bf16(jnp.asarray(g, f32))
    return jnp.asarray(ffwd, bf16), jnp.asarray(finv, bf16), g_hi, g_lo


def _filter_spec_kernel(ghi_ref, glo_ref, lo_ref, hi_ref, s_ref, ha_ref, hb_ref, *, p):
    def mm3(cols, x):
        x_hi, x_lo = _split_bf16(x)
        g_hi = ghi_ref[:, cols]
        return _dot(g_hi, x_hi) + _dot(g_hi, x_lo) + _dot(glo_ref[:, cols], x_hi)

    h = mm3(slice(0, p), lo_ref[...]) + mm3(slice(p, 2 * p), hi_ref[...])
    h = h / s_ref[...]
    ha_ref[0] = h[:p]
    hb_ref[0] = h[p:]


def _filter_spec(n, p, g_hi, g_lo, hraw, hsum):
    nb = n // p
    c2 = 2 * D
    tc = 512
    nd = 2 * nb - 1
    out = jax.ShapeDtypeStruct((nd, p, c2), f32)
    return pl.pallas_call(
        functools.partial(_filter_spec_kernel, p=p),
        grid=(nd, c2 // tc),
        in_specs=[
            _whole(g_hi), _whole(g_lo),
            pl.BlockSpec((p, tc), lambda d, c: (d, c)),
            pl.BlockSpec((p, tc), lambda d, c: (d + 1, c)),
            pl.BlockSpec((1, tc), lambda d, c: (0, c)),
        ],
        out_specs=[pl.BlockSpec((1, p, tc), lambda d, c: (d, 0, c))] * 2,
        out_shape=[out, out],
        compiler_params=_params(("arbitrary", "arbitrary")),
        name="filter_spec",
    )(g_hi, g_lo, hraw, hraw, hsum)


def _hyena_kernel(pv_ref, p1_ref, p2_ref, cw_ref, cb_ref, skip_ref,
                  ha0_ref, hb0_ref, ha1_ref, hb1_ref, ffwd_ref, finv_ref,
                  o_ref, sv_ref, s1_ref, s2_ref, u_ref, za_ref, zb_ref, yc_ref, *, n, p):
    nb = n // p
    tc = HY_TC
    halves = [slice(bb * tc, (bb + 1) * tc) for bb in range(HY_NBAT)]

    def short_conv(p_ref, part, dst_ref):
        w = cw_ref[part]
        w0, w1, w2, bias = w[0:1], w[1:2], w[2:3], cb_ref[part]
        edge = lax.broadcasted_iota(jnp.int32, (8, tc), 0)
        for bb, lanes in enumerate(halves):
            dst_ref[8:n - 8, lanes] = (p_ref[bb, 7:n - 9, :] * w0 + p_ref[bb, 8:n - 8, :] * w1
                                       + p_ref[bb, 9:n - 7, :] * w2 + bias)
            top = p_ref[bb, 0:8, :]
            prev = jnp.where(edge == 0, 0.0, pltpu.roll(top, 1, axis=0))
            dst_ref[0:8, lanes] = prev * w0 + top * w1 + p_ref[bb, 1:9, :] * w2 + bias
            bot = p_ref[bb, n - 8:n, :]
            nxt = jnp.where(edge == 7, 0.0, pltpu.roll(bot, 7, axis=0))
            dst_ref[n - 8:n, lanes] = p_ref[bb, n - 9:n - 1, :] * w0 + bot * w1 + nxt * w2 + bias

    short_conv(pv_ref, 0, sv_ref)
    short_conv(p1_ref, 1, s1_ref)
    short_conv(p2_ref, 2, s2_ref)

    def long_conv(src_ref, gate_ref, ha_ref, hb_ref, skip, store):
        for j in range(nb):
            zt = _dot(ffwd_ref[...], src_ref[j * p:(j + 1) * p, :].astype(bf16))
            za_ref[j] = zt[:p]
            zb_ref[j] = zt[p:]
        for i in range(nb):
            for c in range(p // HY_ROWS):
                rows = slice(c * HY_ROWS, (c + 1) * HY_ROWS)
                ya = [jnp.zeros((HY_ROWS, tc), f32) for _ in halves]
                yb = [jnp.zeros((HY_ROWS, tc), f32) for _ in halves]
                for j in range(nb):
                    d = i - j + (nb - 1)
                    h_a = ha_ref[d, rows, :]
                    h_b = hb_ref[d, rows, :]
                    for bb, lanes in enumerate(halves):
                        z_a = za_ref[j, rows, lanes]
                        z_b = zb_ref[j, rows, lanes]
                        ya[bb] = ya[bb] + h_a * z_a - h_b * z_b
                        yb[bb] = yb[bb] + h_a * z_b + h_b * z_a
                for bb, lanes in enumerate(halves):
                    yc_ref[i, c * HY_ROWS:(c + 1) * HY_ROWS, lanes] = ya[bb].astype(bf16)
                    yc_ref[i, p + c * HY_ROWS:p + (c + 1) * HY_ROWS, lanes] = yb[bb].astype(bf16)
            y = _dot(finv_ref[...], yc_ref[i])
            blk = slice(i * p, (i + 1) * p)
            store(blk, gate_ref[blk, :] * (y + src_ref[blk, :] * skip))

    def store_u(blk, val):
        u_ref[blk, :] = val

    def store_o(blk, val):
        for bb, lanes in enumerate(halves):
            o_ref[bb, blk, :] = val[:, lanes].astype(o_ref.dtype)

    skip0 = jnp.concatenate([skip_ref[0:1, :]] * HY_NBAT, axis=1)
    skip1 = jnp.concatenate([skip_ref[1:2, :]] * HY_NBAT, axis=1)
    long_conv(sv_ref, s1_ref, ha0_ref, hb0_ref, skip0, store_u)
    long_conv(u_ref, s2_ref, ha1_ref, hb1_ref, skip1, store_o)


def _hyena(pb, conv_w3, conv_b3, skip, h_a, h_b, ffwd, finv, p):
    b, n, _ = pb.shape
    tc = HY_TC
    nct = D // tc
    nb = n // p
    nd = 2 * nb - 1
    wide = HY_NBAT * tc

    def pspec(part):
        return pl.BlockSpec((HY_NBAT, n, tc), lambda c, b: (b, 0, part * nct + c))

    def hspec(order):
        return pl.BlockSpec((nd, p, tc), lambda c, b: (0, 0, order * nct + c),
                            pipeline_mode=pl.Buffered(1))

    return pl.pallas_call(
        functools.partial(_hyena_kernel, n=n, p=p),
        grid=(nct, b // HY_NBAT),
        in_specs=[
            pspec(0), pspec(1), pspec(2),
            pl.BlockSpec((3, 3, tc), lambda c, b: (0, 0, c)),
            pl.BlockSpec((3, 1, tc), lambda c, b: (0, 0, c)),
            pl.BlockSpec((2, tc), lambda c, b: (0, c)),
            hspec(0), hspec(0), hspec(1), hspec(1),
            _whole(ffwd), _whole(finv),
        ],
        out_specs=pl.BlockSpec((HY_NBAT, n, tc), lambda c, b: (b, 0, c)),
        out_shape=jax.ShapeDtypeStruct((b, n, D), bf16),
        scratch_shapes=[
            pltpu.VMEM((n, wide), f32), pltpu.VMEM((n, wide), f32), pltpu.VMEM((n, wide), f32),
            pltpu.VMEM((n, wide), f32),
            pltpu.VMEM((nb, p, wide), f32), pltpu.VMEM((nb, p, wide), f32),
            pltpu.VMEM((nb, 2 * p, wide), bf16),
        ],
        compiler_params=_params(("arbitrary", "arbitrary")),
        name="hyena",
    )(pb, pb, pb, conv_w3, conv_b3, skip, h_a, h_b, h_a, h_b, ffwd, finv)


def _mix_ffn_kernel(*refs, na, final):
    x_ref, gate1_ref = refs[:2]
    a_refs = refs[2:2 + na]
    wo_refs = refs[2 + na:2 + 2 * na]
    g_ref, sh_ref, sc_ref, gate2_ref, wgu_ref, wd_ref = refs[2 + 2 * na:8 + 2 * na]
    rest = refs[8 + 2 * na:]
    fg_ref = rest[0] if final else None
    o_ref, x1_ref = rest[-2:]

    mix = _dot(a_refs[0][0], wo_refs[0][...])
    for k in range(1, na):
        mix = mix + _dot(a_refs[k][0], wo_refs[k][...])
    x1_ref[...] = x_ref[0] + gate1_ref[0] * mix

    h = _norm_mod(x1_ref[...], g_ref[...], sc_ref[0], sh_ref[0]).astype(bf16)
    acc = None
    for f in range(D_FF // FFN_TF):
        cols = slice(f * FFN_TF, (f + 1) * FFN_TF)
        ucols = slice(D_FF + f * FFN_TF, D_FF + (f + 1) * FFN_TF)
        act = (jax.nn.silu(_dot(h, wgu_ref[:, cols])) * _dot(h, wgu_ref[:, ucols])).astype(bf16)
        part = _dot(act, wd_ref[cols, :])
        acc = part if acc is None else acc + part
    y = x1_ref[...] + gate2_ref[0] * acc
    if final:
        y = y * lax.rsqrt(jnp.mean(y * y, axis=-1, keepdims=True) + EPS) * fg_ref[...]
    o_ref[0] = y


def _mix_ffn(x, mods3, row_of, acts, ws, g, w_gu, w_down, tm, final_g=None):
    b, n, _ = x.shape
    na = len(acts)
    extra = [] if final_g is None else [final_g]
    return pl.pallas_call(
        functools.partial(_mix_ffn_kernel, na=na, final=final_g is not None),
        grid=(b, n // tm),
        in_specs=([_row_spec(tm, D), _mod_spec(row_of, 2)]
                  + [_row_spec(tm, a.shape[-1]) for a in acts] + [_whole(w) for w in ws]
                  + [_whole(g), _mod_spec(row_of, 3), _mod_spec(row_of, 4), _mod_spec(row_of, 5),
                     _whole(w_gu), _whole(w_down)] + [_whole(a) for a in extra]),
        out_specs=_row_spec(tm, D),
        out_shape=jax.ShapeDtypeStruct((b, n, D), f32),
        scratch_shapes=[pltpu.VMEM((tm, D), f32)],
        compiler_params=_params(("arbitrary", "arbitrary")),
        name="mix_ffn",
    )(x, mods3, *acts, *ws, g, mods3, mods3, mods3, w_gu, w_down, *extra)


def _rope_tables(n):
    rows = n // GRID_W
    row = np.repeat(np.arange(rows), GRID_W).astype(np.float64)
    col = np.tile(np.arange(GRID_W), rows).astype(np.float64)
    half = HEAD_DIM // 2
    inv = ROPE_THETA ** (-np.arange(0, half, 2, dtype=np.float64) / half)
    ang = np.concatenate([row[:, None] * inv, col[:, None] * inv], axis=-1)
    cos, sin = np.cos(ang), np.sin(ang)
    return (jnp.asarray(np.concatenate([cos, cos], axis=-1), f32),
            jnp.asarray(np.concatenate([-sin, sin], axis=-1), f32))


def _qkv_kernel(*refs, with_q, rope):
    x_ref, g_ref, sh_ref, sc_ref, w_ref, qg_ref, kg_ref = refs[:7]
    pos = 7
    if rope:
        cos_ref, sin_ref = refs[7:9]
        pos = 9
    out_refs = refs[pos:]
    h = _norm_mod(x_ref[0], g_ref[...], sc_ref[0], sh_ref[0]).astype(bf16)
    qkv = _dot(h, w_ref[...])
    ones = jnp.ones((HEAD_DIM, HEAD_DIM), bf16)

    def head(t, gain, scale):
        sq_hi, sq_lo = _split_bf16(t * t)
        ssq = _dot(sq_hi, ones) + _dot(sq_lo, ones)
        t = t * lax.rsqrt(ssq * (1.0 / HEAD_DIM) + EPS) * gain
        if rope:
            t = t * cos_ref[...] + pltpu.roll(t, HEAD_DIM // 2, axis=1) * sin_ref[...]
        return (t * scale).astype(bf16)

    off = 0
    if with_q:
        q_ref, k_ref, v_ref = out_refs
        for hd in range(N_HEADS):
            cols = slice(hd * HEAD_DIM, (hd + 1) * HEAD_DIM)
            q_ref[0, :, cols] = head(qkv[:, cols], qg_ref[...], HEAD_DIM ** -0.5)
        off = N_HEADS * HEAD_DIM
    else:
        k_ref, v_ref = out_refs
    for kv in range(N_KV):
        cols = slice(kv * HEAD_DIM, (kv + 1) * HEAD_DIM)
        src = slice(off + kv * HEAD_DIM, off + (kv + 1) * HEAD_DIM)
        k_ref[0, :, cols] = head(qkv[:, src], kg_ref[...], 1.0)
    nk = N_KV * HEAD_DIM
    v_ref[0] = qkv[:, off + nk:off + 2 * nk].astype(bf16)


def _qkv(x, g, mods3, row_of, w, q_g, k_g, tables, with_q, tm):
    b, n, _ = x.shape
    nq = N_HEADS * HEAD_DIM
    nk = N_KV * HEAD_DIM
    rope = tables is not None
    in_specs = [_row_spec(tm, D), _whole(g), _mod_spec(row_of, 0), _mod_spec(row_of, 1),
                _whole(w), _whole(q_g), _whole(k_g)]
    args = [x, g, mods3, mods3, w, q_g, k_g]
    if rope:
        in_specs += [pl.BlockSpec((tm, HEAD_DIM), lambda b, i: (i, 0))] * 2
        args += list(tables)
    out_specs = [_row_spec(tm, nk)] * 2
    out_shape = [jax.ShapeDtypeStruct((b, n, nk), bf16)] * 2
    if with_q:
        out_specs = [_row_spec(tm, nq)] + out_specs
        out_shape = [jax.ShapeDtypeStruct((b, n, nq), bf16)] + out_shape
    return pl.pallas_call(
        functools.partial(_qkv_kernel, with_q=with_q, rope=rope),
        grid=(b, n // tm),
        in_specs=in_specs,
        out_specs=out_specs,
        out_shape=out_shape,
        compiler_params=_params(("arbitrary", "arbitrary")),
        name="qkv",
    )(*args)


def _attn_kernel(*refs, nseg):
    q_ref = refs[0]
    k_refs = refs[1:1 + nseg]
    v_refs = refs[1 + nseg:1 + 2 * nseg]
    o_ref = refs[1 + 2 * nseg]
    for hd in range(GQA):
        cols = slice(hd * HEAD_DIM, (hd + 1) * HEAD_DIM)
        q = q_ref[0, :, cols]
        s = [lax.dot_general(q, k[0], _NT, preferred_element_type=f32) for k in k_refs]
        m = jnp.max(s[0], axis=-1, keepdims=True)
        for t in s[1:]:
            m = jnp.maximum(m, jnp.max(t, axis=-1, keepdims=True))
        e = [jnp.exp(t - m) for t in s]
        l = jnp.sum(e[0], axis=-1, keepdims=True)
        o = _dot(e[0].astype(bf16), v_refs[0][0])
        for t, v in zip(e[1:], v_refs[1:]):
            l = l + jnp.sum(t, axis=-1, keepdims=True)
            o = o + _dot(t.astype(bf16), v[0])
        o_ref[0, :, cols] = (o / l).astype(o_ref.dtype)


def _attention(q, ks, vs, tq):
    b, n, _ = q.shape
    nseg = len(ks)
    gw = GQA * HEAD_DIM
    kv_specs = [pl.BlockSpec((1, k.shape[1], HEAD_DIM), lambda b, h, i: (b, 0, h)) for k in ks]
    return pl.pallas_call(
        functools.partial(_attn_kernel, nseg=nseg),
        grid=(b, N_KV, n // tq),
        in_specs=[pl.BlockSpec((1, tq, gw), lambda b, h, i: (b, i, h))] + kv_specs + kv_specs,
        out_specs=pl.BlockSpec((1, tq, gw), lambda b, h, i: (b, i, h)),
        out_shape=jax.ShapeDtypeStruct((b, n, N_HEADS * HEAD_DIM), bf16),
        compiler_params=_params(("arbitrary", "arbitrary", "arbitrary")),
        name="attn",
    )(q, *ks, *vs)


def _pad2(a, rows, cols):
    return jnp.pad(a, ((0, rows - a.shape[0]), (0, cols - a.shape[1])))


def kernel(x, c, ctx, c_ctx, mod_w, mod_b, norm1_g, norm2_g, ffn_w_gu, ffn_w_down, even_w_in,
           gmlp_ln_g, gmlp_ln_b, gmlp_w_s, gmlp_b_s, hyena_conv_w, hyena_conv_b, hyena_f_w1,
           hyena_f_b1, hyena_f_w2, hyena_f_b2, hyena_f_w3, hyena_freq, hyena_skip, even_w_out,
           attn_w_qkv, attn_q_g, attn_k_g, attn_w_o, final_g):
    lat_row = lambda b: b
    ctx_row = lambda b: CTX_ROW
    streams = {"lat": (SEQ, lat_row, 512), "ctx": (CTX, ctx_row, 512)}

    cond = jnp.zeros((MOD_ROWS, D), f32).at[:NB].set(c).at[CTX_ROW].set(c_ctx)
    mods = _mods(cond, mod_w, mod_b)

    dft = {p: _dft_consts(p) for p in set(HY_P.values())}
    fconst = {n: _filter_consts(n) for n in (SEQ, CTX)}
    rope = _rope_tables(SEQ)
    nq = N_HEADS * HEAD_DIM
    nk = N_KV * HEAD_DIM

    xs = {"lat": x, "ctx": ctx.reshape(1, NB * CTX, D)}
    for layer in range(DEPTH):
        last = layer == DEPTH - 1
        is_even = layer % 2 == 0
        mods3 = mods[layer].reshape(MOD_ROWS, 1, 6 * D)
        g1 = norm1_g[layer].reshape(1, D)
        g2 = norm2_g[layer].reshape(1, D)
        w_gu = ffn_w_gu[layer].astype(bf16)
        w_down = ffn_w_down[layer].astype(bf16)
        mixed = {}
        if is_even:
            i = layer // 2
            w_in = even_w_in[i].astype(bf16)
            w_out = even_w_out[i].astype(bf16)
            w_s = gmlp_w_s[i].astype(bf16)
            b_s_rows = jnp.repeat(gmlp_b_s[i].T, CHUNK, axis=1)
            ln_g = gmlp_ln_g[i].reshape(1, D)
            ln_b = gmlp_ln_b[i].reshape(1, D)
            conv_w3 = hyena_conv_w[i].reshape(3, 3, D).transpose(1, 0, 2)
            conv_b3 = hyena_conv_b[i].reshape(3, 1, D)
            w1p = _pad2(hyena_f_w1[i], LANES, LANES)
            w2p = _pad2(hyena_f_w2[i], LANES, LANES)
            b1p = _pad2(hyena_f_b1[i][None, :], 1, LANES)
            b2p = _pad2(hyena_f_b2[i][None, :], 1, LANES)
            f0p = _pad2(hyena_freq[i, 0][None, :], 1, LANES)
            f1p = _pad2(hyena_freq[i, 1][None, :], 1, LANES)
            w3p = jnp.pad(hyena_f_w3[i].reshape(FILTER_HIDDEN, 2, 2 * D).transpose(1, 0, 2),
                          ((0, 0), (0, LANES - FILTER_HIDDEN), (0, 0)))
            for key, (n, row_of, tm) in streams.items():
                if key == "ctx" and last:
                    continue
                xc = xs[key]
                p = HY_P[n]
                ffwd, finv, g_hi, g_lo = dft[p]
                zemb, deltas = fconst[n]
                hraw, hsum = _filter_time(n, zemb, deltas, w1p, b1p, w2p, b2p, f0p, f1p, w3p)
                h_a, h_b = _filter_spec(n, p, g_hi, g_lo, hraw, hsum)
                y_a, pb = _even_in(xc, g1, mods3, row_of, w_in, ln_g, ln_b, w_s, b_s_rows, tm)
                z_b = _hyena(pb.reshape(NB, n, 3 * D), conv_w3, conv_b3, hyena_skip[i],
                             h_a, h_b, ffwd, finv, p)
                mixed[key] = ([y_a, z_b.reshape(y_a.shape)], [w_out[:D], w_out[D:]])
        else:
            j = layer // 2
            w_qkv = attn_w_qkv[j].astype(bf16)
            w_o = attn_w_o[j].astype(bf16)
            q_g = attn_q_g[j].reshape(1, HEAD_DIM)
            k_g = attn_k_g[j].reshape(1, HEAD_DIM)
            q_l, k_l, v_l = _qkv(xs["lat"], g1, mods3, lat_row, w_qkv, q_g, k_g, rope, True, 512)
            if last:
                k_c, v_c = _qkv(xs["ctx"], g1, mods3, ctx_row, w_qkv[:, nq:], q_g, k_g, None, False, 512)
            else:
                q_c, k_c, v_c = _qkv(xs["ctx"], g1, mods3, ctx_row, w_qkv, q_g, k_g, None, True, 512)
            k_c = k_c.reshape(NB, CTX, nk)
            v_c = v_c.reshape(NB, CTX, nk)
            mixed["lat"] = ([_attention(q_l, [k_c, k_l], [v_c, v_l], 256)], [w_o])
            if not last:
                o_c = _attention(q_c.reshape(NB, CTX, nq), [k_c], [v_c], CTX)
                mixed["ctx"] = ([o_c.reshape(1, NB * CTX, nq)], [w_o])
        for key, (acts, ws) in mixed.items():
            _, row_of, tm = streams[key]
            fg = final_g.reshape(1, D) if last else None
            xs[key] = _mix_ffn(xs[key], mods3, row_of, acts, ws, g2, w_gu, w_down, tm, fg)
    return xs["lat"]
```

```python
import functools
import math

import numpy as np
import jax
import jax.numpy as jnp
from jax import lax
from jax.experimental import pallas as pl
from jax.experimental.pallas import tpu as pltpu

D = 1024
NB = 16
SEQ = 2048
CTX = 256
DEPTH = 4
GRID_W = 64
EPS = 1e-6
CHUNK = 128
A_GROUPS = 8
HEAD_DIM = 128
N_HEADS = 8
N_KV = 2
GQA = N_HEADS // N_KV
D_FF = 2816
FILTER_EMB = 33
FILTER_BANDS = 16
FILTER_HIDDEN = 64
MIN_DECAY = math.log(1e-2) / 1.5
MAX_DECAY = math.log(1e-2) / 0.3
ROPE_THETA = 10000.0

LANES = 128
CTX_ROW = NB
MOD_ROWS = 24
HY_P = {SEQ: 512, CTX: 256}
HY_TC = LANES
HY_NBAT = 2
HY_ROWS = 32
FFN_TF = 256
VMEM_LIMIT = 56 * 1024 * 1024

bf16 = jnp.bfloat16
f32 = jnp.float32


_NT = (((1,), (1,)), ((), ()))


def _dot(a, b):
    return jnp.dot(a, b, preferred_element_type=f32)


def _mm_f32(a, b):
    return jnp.dot(a, b, preferred_element_type=f32, precision=lax.Precision.HIGHEST)


def _params(sem):
    return pltpu.CompilerParams(dimension_semantics=sem, vmem_limit_bytes=VMEM_LIMIT)


def _whole(a):
    nd = a.ndim
    return pl.BlockSpec(a.shape, lambda *_: (0,) * nd)


def _mods_kernel(cond_ref, w_ref, b_ref, o_ref):
    a = jax.nn.silu(cond_ref[...]).astype(bf16)
    o_ref[0] = _dot(a, w_ref[0].astype(bf16)) + b_ref[0]


def _mods(cond, mod_w, mod_b):
    tn = 1024
    n6 = 6 * D
    return pl.pallas_call(
        _mods_kernel,
        grid=(DEPTH, n6 // tn),
        in_specs=[
            pl.BlockSpec((MOD_ROWS, D), lambda l, j: (0, 0)),
            pl.BlockSpec((1, D, tn), lambda l, j: (l, 0, j)),
            pl.BlockSpec((1, 1, tn), lambda l, j: (l, 0, j)),
        ],
        out_specs=pl.BlockSpec((1, MOD_ROWS, tn), lambda l, j: (l, 0, j)),
        out_shape=jax.ShapeDtypeStruct((DEPTH, MOD_ROWS, n6), f32),
        compiler_params=_params(("arbitrary", "arbitrary")),
        name="mods",
    )(cond, mod_w, mod_b.reshape(DEPTH, 1, n6))


def _mod_spec(row_of, chunk):
    return pl.BlockSpec((1, 1, D), lambda b, i: (row_of(b), 0, chunk))


def _row_spec(tm, width):
    return pl.BlockSpec((1, tm, width), lambda b, i: (b, i, 0))


def _norm_mod(x, g, sc, sh):
    y = x * lax.rsqrt(jnp.mean(x * x, axis=-1, keepdims=True) + EPS)
    return (y * g) * (1.0 + sc) + sh


def _gelu(x):
    return 0.5 * x * (1.0 + lax.erf(x * (2.0 ** -0.5)))


def _even_in_kernel(x_ref, g_ref, sh_ref, sc_ref, w_ref, lng_ref, lnb_ref, ws_ref, bs_ref,
                    ya_ref, pb_ref, *, tm):
    h = _norm_mod(x_ref[0], g_ref[...], sc_ref[0], sh_ref[0]).astype(bf16)
    pb_ref[0] = _dot(h, w_ref[:, 2 * D:])
    u = _gelu(_dot(h, w_ref[:, :D]))
    v = _gelu(_dot(h, w_ref[:, D:2 * D]))
    mu = jnp.mean(v, axis=-1, keepdims=True)
    vc = v - mu
    var = jnp.mean(vc * vc, axis=-1, keepdims=True)
    vn = (vc * lax.rsqrt(var + 1e-5) * lng_ref[...] + lnb_ref[...]).astype(bf16)
    nck = tm // CHUNK
    for g in range(A_GROUPS):
        cols = slice(g * CHUNK, (g + 1) * CHUNK)
        rhs = jnp.concatenate([vn[k * CHUNK:(k + 1) * CHUNK, cols] for k in range(nck)], axis=1)
        sv = _dot(ws_ref[g], rhs)
        for k in range(nck):
            rows = slice(k * CHUNK, (k + 1) * CHUNK)
            gate = sv[:, k * CHUNK:(k + 1) * CHUNK] + bs_ref[:, cols]
            ya_ref[0, rows, cols] = (u[rows, cols] * gate).astype(bf16)


def _even_in(x, g, mods3, row_of, w_in, ln_g, ln_b, w_s, b_s_rows, tm):
    b, n, _ = x.shape
    return pl.pallas_call(
        functools.partial(_even_in_kernel, tm=tm),
        grid=(b, n // tm),
        in_specs=[
            _row_spec(tm, D), _whole(g), _mod_spec(row_of, 0), _mod_spec(row_of, 1),
            _whole(w_in), _whole(ln_g), _whole(ln_b), _whole(w_s), _whole(b_s_rows),
        ],
        out_specs=[_row_spec(tm, D), _row_spec(tm, 3 * D)],
        out_shape=[jax.ShapeDtypeStruct((b, n, D), bf16), jax.ShapeDtypeStruct((b, n, 3 * D), f32)],
        compiler_params=_params(("arbitrary", "arbitrary")),
        name="even_in",
    )(x, g, mods3, mods3, w_in, ln_g, ln_b, w_s, b_s_rows)


def _filter_consts(n):
    lag = np.abs(np.arange(2 * n) - n)
    lag[0] = 0
    t = np.linspace(0.0, 1.0, n)[lag][:, None]
    w = (2.0 * math.pi / n) * lag[:, None]
    bands = np.linspace(1e-4, FILTER_BANDS - 1, FILTER_BANDS)[None, :]
    ang = bands * w
    z = np.concatenate([t, np.cos(ang), -np.sin(ang)], axis=-1)
    zp = np.zeros((2 * n, LANES), np.float64)
    zp[:, :FILTER_EMB] = z
    deltas = np.abs(np.linspace(MIN_DECAY, MAX_DECAY, D))
    return jnp.asarray(zp, f32), jnp.asarray(np.tile(deltas, 2)[None, :], f32)


def _filter_time_kernel(z_ref, w1_ref, b1_ref, w2_ref, b2_ref, f0_ref, f1_ref, w3_ref, dl_ref,
                        h_ref, s_ref, *, tr):
    r = pl.program_id(0)
    z = z_ref[...]
    hid = jnp.sin(f0_ref[...] * (_mm_f32(z, w1_ref[...]) + b1_ref[...]))
    hid = jnp.sin(f1_ref[...] * (_mm_f32(hid, w2_ref[...]) + b2_ref[...]))
    h = _mm_f32(hid, w3_ref[0])
    h = h * jnp.exp(-z[:, 0:1] * dl_ref[...])
    row = lax.broadcasted_iota(jnp.int32, h.shape, 0) + r * tr
    h = jnp.where(row == 0, 0.0, h)
    h_ref[...] = h

    @pl.when(r == 0)
    def _():
        s_ref[...] = jnp.zeros_like(s_ref)

    s_ref[...] += jnp.sum(jnp.abs(h), axis=0, keepdims=True)


def _filter_time(n, zemb, deltas, w1p, b1p, w2p, b2p, f0p, f1p, w3p):
    tr = 256
    half = n // tr
    c2 = 2 * D
    return pl.pallas_call(
        functools.partial(_filter_time_kernel, tr=tr),
        grid=(2 * n // tr,),
        in_specs=[
            pl.BlockSpec((tr, LANES), lambda r: (r, 0)),
            _whole(w1p), _whole(b1p), _whole(w2p), _whole(b2p), _whole(f0p), _whole(f1p),
            pl.BlockSpec((1, LANES, c2), lambda r: (jnp.where(r < half, 1, 0), 0, 0)),
            _whole(deltas),
        ],
        out_specs=[pl.BlockSpec((tr, c2), lambda r: (r, 0)), pl.BlockSpec((1, c2), lambda r: (0, 0))],
        out_shape=[jax.ShapeDtypeStruct((2 * n, c2), f32), jax.ShapeDtypeStruct((1, c2), f32)],
        compiler_params=_params(("arbitrary",)),
        name="filter_time",
    )(zemb, w1p, b1p, w2p, b2p, f0p, f1p, w3p, deltas)


def _split_bf16(a):
    hi = a.astype(bf16)
    return hi, (a - hi.astype(f32)).astype(bf16)


def _dft_consts(p):
    k = np.arange(p)
    theta = np.pi * (2 * k + 1) / (2 * p)
    s = np.arange(p)
    c = np.cos(theta[:, None] * s[None, :])
    sn = np.sin(theta[:, None] * s[None, :])
    ffwd = np.concatenate([c, -sn], axis=0)
    finv = np.concatenate([c.T, -sn.T], axis=1) / p
    m = np.arange(-p, p)
    g = np.concatenate([np.cos(theta[:, None] * m[None, :]),
                        -np.sin(theta[:, None] * m[None, :])], axis=0)
    g[:, 0] = 0.0
    return jnp.asarray(ffwd, bf16), jnp.asarray(finv, bf16), jnp.asarray(g, bf16)


def _filter_spec_kernel(g_ref, lo_ref, hi_ref, s_ref, ha_ref, hb_ref, *, p):
    h = (_dot(g_ref[:, :p], lo_ref[...].astype(bf16))
         + _dot(g_ref[:, p:], hi_ref[...].astype(bf16)))
    h = h / s_ref[...]
    ha_ref[0] = h[:p]
    hb_ref[0] = h[p:]


def _filter_spec(n, p, gmat, hraw, hsum):
    nb = n // p
    c2 = 2 * D
    tc = 512
    nd = 2 * nb - 1
    out = jax.ShapeDtypeStruct((nd, p, c2), f32)
    return pl.pallas_call(
        functools.partial(_filter_spec_kernel, p=p),
        grid=(nd, c2 // tc),
        in_specs=[
            _whole(gmat),
            pl.BlockSpec((p, tc), lambda d, c: (d, c)),
            pl.BlockSpec((p, tc), lambda d, c: (d + 1, c)),
            pl.BlockSpec((1, tc), lambda d, c: (0, c)),
        ],
        out_specs=[pl.BlockSpec((1, p, tc), lambda d, c: (d, 0, c))] * 2,
        out_shape=[out, out],
        compiler_params=_params(("arbitrary", "arbitrary")),
        name="filter_spec",
    )(gmat, hraw, hraw, hsum)


def _hyena_kernel(pv_ref, p1_ref, p2_ref, cw_ref, cb_ref, skip_ref,
                  ha0_ref, hb0_ref, ha1_ref, hb1_ref, ffwd_ref, finv_ref,
                  o_ref, sv_ref, s1_ref, s2_ref, u_ref, za_ref, zb_ref, yc_ref, *, n, p):
    nb = n // p
    tc = HY_TC
    halves = [slice(bb * tc, (bb + 1) * tc) for bb in range(HY_NBAT)]

    def short_conv(p_ref, part, dst_ref):
        w = cw_ref[part]
        w0, w1, w2, bias = w[0:1], w[1:2], w[2:3], cb_ref[part]
        edge = lax.broadcasted_iota(jnp.int32, (8, tc), 0)
        for bb, lanes in enumerate(halves):
            dst_ref[8:n - 8, lanes] = (p_ref[bb, 7:n - 9, :] * w0 + p_ref[bb, 8:n - 8, :] * w1
                                       + p_ref[bb, 9:n - 7, :] * w2 + bias)
            top = p_ref[bb, 0:8, :]
            prev = jnp.where(edge == 0, 0.0, pltpu.roll(top, 1, axis=0))
            dst_ref[0:8, lanes] = prev * w0 + top * w1 + p_ref[bb, 1:9, :] * w2 + bias
            bot = p_ref[bb, n - 8:n, :]
            nxt = jnp.where(edge == 7, 0.0, pltpu.roll(bot, 7, axis=0))
            dst_ref[n - 8:n, lanes] = p_ref[bb, n - 9:n - 1, :] * w0 + bot * w1 + nxt * w2 + bias

    short_conv(pv_ref, 0, sv_ref)
    short_conv(p1_ref, 1, s1_ref)
    short_conv(p2_ref, 2, s2_ref)

    def long_conv(src_ref, gate_ref, ha_ref, hb_ref, skip, store):
        for j in range(nb):
            zt = _dot(ffwd_ref[...], src_ref[j * p:(j + 1) * p, :].astype(bf16))
            za_ref[j] = zt[:p]
            zb_ref[j] = zt[p:]
        for i in range(nb):
            for c in range(p // HY_ROWS):
                rows = slice(c * HY_ROWS, (c + 1) * HY_ROWS)
                ya = [jnp.zeros((HY_ROWS, tc), f32) for _ in halves]
                yb = [jnp.zeros((HY_ROWS, tc), f32) for _ in halves]
                for j in range(nb):
                    d = i - j + (nb - 1)
                    h_a = ha_ref[d, rows, :]
                    h_b = hb_ref[d, rows, :]
                    for bb, lanes in enumerate(halves):
                        z_a = za_ref[j, rows, lanes]
                        z_b = zb_ref[j, rows, lanes]
                        ya[bb] = ya[bb] + h_a * z_a - h_b * z_b
                        yb[bb] = yb[bb] + h_a * z_b + h_b * z_a
                for bb, lanes in enumerate(halves):
                    yc_ref[i, c * HY_ROWS:(c + 1) * HY_ROWS, lanes] = ya[bb].astype(bf16)
                    yc_ref[i, p + c * HY_ROWS:p + (c + 1) * HY_ROWS, lanes] = yb[bb].astype(bf16)
            y = _dot(finv_ref[...], yc_ref[i])
            blk = slice(i * p, (i + 1) * p)
            store(blk, gate_ref[blk, :] * (y + src_ref[blk, :] * skip))

    def store_u(blk, val):
        u_ref[blk, :] = val

    def store_o(blk, val):
        for bb, lanes in enumerate(halves):
            o_ref[bb, blk, :] = val[:, lanes].astype(o_ref.dtype)

    skip0 = jnp.concatenate([skip_ref[0:1, :]] * HY_NBAT, axis=1)
    skip1 = jnp.concatenate([skip_ref[1:2, :]] * HY_NBAT, axis=1)
    long_conv(sv_ref, s1_ref, ha0_ref, hb0_ref, skip0, store_u)
    long_conv(u_ref, s2_ref, ha1_ref, hb1_ref, skip1, store_o)


def _hyena(pb, conv_w3, conv_b3, skip, h_a, h_b, ffwd, finv, p):
    b, n, _ = pb.shape
    tc = HY_TC
    nct = D // tc
    nb = n // p
    nd = 2 * nb - 1
    wide = HY_NBAT * tc

    def pspec(part):
        return pl.BlockSpec((HY_NBAT, n, tc), lambda c, b: (b, 0, part * nct + c))

    def hspec(order):
        return pl.BlockSpec((nd, p, tc), lambda c, b: (0, 0, order * nct + c),
                            pipeline_mode=pl.Buffered(1))

    return pl.pallas_call(
        functools.partial(_hyena_kernel, n=n, p=p),
        grid=(nct, b // HY_NBAT),
        in_specs=[
            pspec(0), pspec(1), pspec(2),
            pl.BlockSpec((3, 3, tc), lambda c, b: (0, 0, c)),
            pl.BlockSpec((3, 1, tc), lambda c, b: (0, 0, c)),
            pl.BlockSpec((2, tc), lambda c, b: (0, c)),
            hspec(0), hspec(0), hspec(1), hspec(1),
            _whole(ffwd), _whole(finv),
        ],
        out_specs=pl.BlockSpec((HY_NBAT, n, tc), lambda c, b: (b, 0, c)),
        out_shape=jax.ShapeDtypeStruct((b, n, D), bf16),
        scratch_shapes=[
            pltpu.VMEM((n, wide), f32), pltpu.VMEM((n, wide), f32), pltpu.VMEM((n, wide), f32),
            pltpu.VMEM((n, wide), f32),
            pltpu.VMEM((nb, p, wide), f32), pltpu.VMEM((nb, p, wide), f32),
            pltpu.VMEM((nb, 2 * p, wide), bf16),
        ],
        compiler_params=_params(("arbitrary", "arbitrary")),
        name="hyena",
    )(pb, pb, pb, conv_w3, conv_b3, skip, h_a, h_b, h_a, h_b, ffwd, finv)


def _mix_ffn_kernel(*refs, na, final):
    x_ref, gate1_ref = refs[:2]
    a_refs = refs[2:2 + na]
    wo_refs = refs[2 + na:2 + 2 * na]
    g_ref, sh_ref, sc_ref, gate2_ref, wgu_ref, wd_ref = refs[2 + 2 * na:8 + 2 * na]
    rest = refs[8 + 2 * na:]
    fg_ref = rest[0] if final else None
    o_ref, x1_ref = rest[-2:]

    mix = _dot(a_refs[0][0], wo_refs[0][...])
    for k in range(1, na):
        mix = mix + _dot(a_refs[k][0], wo_refs[k][...])
    x1_ref[...] = x_ref[0] + gate1_ref[0] * mix

    h = _norm_mod(x1_ref[...], g_ref[...], sc_ref[0], sh_ref[0]).astype(bf16)
    acc = None
    for f in range(D_FF // FFN_TF):
        cols = slice(f * FFN_TF, (f + 1) * FFN_TF)
        ucols = slice(D_FF + f * FFN_TF, D_FF + (f + 1) * FFN_TF)
        act = (jax.nn.silu(_dot(h, wgu_ref[:, cols])) * _dot(h, wgu_ref[:, ucols])).astype(bf16)
        part = _dot(act, wd_ref[cols, :])
        acc = part if acc is None else acc + part
    y = x1_ref[...] + gate2_ref[0] * acc
    if final:
        y = y * lax.rsqrt(jnp.mean(y * y, axis=-1, keepdims=True) + EPS) * fg_ref[...]
    o_ref[0] = y


def _mix_ffn(x, mods3, row_of, acts, ws, g, w_gu, w_down, tm, final_g=None):
    b, n, _ = x.shape
    na = len(acts)
    extra = [] if final_g is None else [final_g]
    return pl.pallas_call(
        functools.partial(_mix_ffn_kernel, na=na, final=final_g is not None),
        grid=(b, n // tm),
        in_specs=([_row_spec(tm, D), _mod_spec(row_of, 2)]
                  + [_row_spec(tm, a.shape[-1]) for a in acts] + [_whole(w) for w in ws]
                  + [_whole(g), _mod_spec(row_of, 3), _mod_spec(row_of, 4), _mod_spec(row_of, 5),
                     _whole(w_gu), _whole(w_down)] + [_whole(a) for a in extra]),
        out_specs=_row_spec(tm, D),
        out_shape=jax.ShapeDtypeStruct((b, n, D), f32),
        scratch_shapes=[pltpu.VMEM((tm, D), f32)],
        compiler_params=_params(("arbitrary", "arbitrary")),
        name="mix_ffn",
    )(x, mods3, *acts, *ws, g, mods3, mods3, mods3, w_gu, w_down, *extra)


def _rope_tables(n):
    rows = n // GRID_W
    row = np.repeat(np.arange(rows), GRID_W).astype(np.float64)
    col = np.tile(np.arange(GRID_W), rows).astype(np.float64)
    half = HEAD_DIM // 2
    inv = ROPE_THETA ** (-np.arange(0, half, 2, dtype=np.float64) / half)
    ang = np.concatenate([row[:, None] * inv, col[:, None] * inv], axis=-1)
    cos, sin = np.cos(ang), np.sin(ang)
    return (jnp.asarray(np.concatenate([cos, cos], axis=-1), f32),
            jnp.asarray(np.concatenate([-sin, sin], axis=-1), f32))


def _qkv_kernel(*refs, with_q, rope):
    x_ref, g_ref, sh_ref, sc_ref, w_ref, qg_ref, kg_ref = refs[:7]
    pos = 7
    if rope:
        cos_ref, sin_ref = refs[7:9]
        pos = 9
    out_refs = refs[pos:]
    h = _norm_mod(x_ref[0], g_ref[...], sc_ref[0], sh_ref[0]).astype(bf16)
    qkv = _dot(h, w_ref[...])
    ones = jnp.ones((HEAD_DIM, HEAD_DIM), bf16)

    def head(t, gain, scale):
        sq_hi, sq_lo = _split_bf16(t * t)
        ssq = _dot(sq_hi, ones) + _dot(sq_lo, ones)
        t = t * lax.rsqrt(ssq * (1.0 / HEAD_DIM) + EPS) * gain
        if rope:
            t = t * cos_ref[...] + pltpu.roll(t, HEAD_DIM // 2, axis=1) * sin_ref[...]
        return (t * scale).astype(bf16)

    off = 0
    if with_q:
        q_ref, k_ref, v_ref = out_refs
        for hd in range(N_HEADS):
            cols = slice(hd * HEAD_DIM, (hd + 1) * HEAD_DIM)
            q_ref[0, :, cols] = head(qkv[:, cols], qg_ref[...], HEAD_DIM ** -0.5)
        off = N_HEADS * HEAD_DIM
    else:
        k_ref, v_ref = out_refs
    for kv in range(N_KV):
        cols = slice(kv * HEAD_DIM, (kv + 1) * HEAD_DIM)
        src = slice(off + kv * HEAD_DIM, off + (kv + 1) * HEAD_DIM)
        k_ref[0, :, cols] = head(qkv[:, src], kg_ref[...], 1.0)
    voff = off + N_KV * HEAD_DIM
    for kv in range(N_KV):
        src = slice(voff + kv * HEAD_DIM, voff + (kv + 1) * HEAD_DIM)
        v_ref[0, :, 2 * kv * HEAD_DIM:(2 * kv + 1) * HEAD_DIM] = qkv[:, src].astype(bf16)
        v_ref[0, :, (2 * kv + 1) * HEAD_DIM:(2 * kv + 2) * HEAD_DIM] = jnp.ones(
            (qkv.shape[0], HEAD_DIM), bf16)


def _qkv(x, g, mods3, row_of, w, q_g, k_g, tables, with_q, tm):
    b, n, _ = x.shape
    nq = N_HEADS * HEAD_DIM
    nk = N_KV * HEAD_DIM
    rope = tables is not None
    in_specs = [_row_spec(tm, D), _whole(g), _mod_spec(row_of, 0), _mod_spec(row_of, 1),
                _whole(w), _whole(q_g), _whole(k_g)]
    args = [x, g, mods3, mods3, w, q_g, k_g]
    if rope:
        in_specs += [pl.BlockSpec((tm, HEAD_DIM), lambda b, i: (i, 0))] * 2
        args += list(tables)
    out_specs = [_row_spec(tm, nk), _row_spec(tm, 2 * nk)]
    out_shape = [jax.ShapeDtypeStruct((b, n, nk), bf16), jax.ShapeDtypeStruct((b, n, 2 * nk), bf16)]
    if with_q:
        out_specs = [_row_spec(tm, nq)] + out_specs
        out_shape = [jax.ShapeDtypeStruct((b, n, nq), bf16)] + out_shape
    return pl.pallas_call(
        functools.partial(_qkv_kernel, with_q=with_q, rope=rope),
        grid=(b, n // tm),
        in_specs=in_specs,
        out_specs=out_specs,
        out_shape=out_shape,
        compiler_params=_params(("arbitrary", "arbitrary")),
        name="qkv",
    )(*args)


def _attn_kernel(*refs, nseg):
    q_ref = refs[0]
    k_refs = refs[1:1 + nseg]
    v_refs = refs[1 + nseg:1 + 2 * nseg]
    o_ref = refs[1 + 2 * nseg]
    def scores(hd):
        q = q_ref[0, :, hd * HEAD_DIM:(hd + 1) * HEAD_DIM]
        return [lax.dot_general(q, k[0], _NT, preferred_element_type=f32) for k in k_refs]

    s_next = scores(0)
    for hd in range(GQA):
        s = s_next
        if hd + 1 < GQA:
            s_next = scores(hd + 1)
        m = jnp.max(s[0], axis=-1, keepdims=True)
        for t in s[1:]:
            m = jnp.maximum(m, jnp.max(t, axis=-1, keepdims=True))
        ov = _dot(jnp.exp(s[0] - m).astype(bf16), v_refs[0][0])
        for t, v in zip(s[1:], v_refs[1:]):
            ov = ov + _dot(jnp.exp(t - m).astype(bf16), v[0])
        o_ref[0, :, hd * HEAD_DIM:(hd + 1) * HEAD_DIM] = (
            ov[:, :HEAD_DIM] / ov[:, HEAD_DIM:]).astype(o_ref.dtype)


def _attention(q, ks, vs, tq):
    b, n, _ = q.shape
    nseg = len(ks)
    gw = GQA * HEAD_DIM
    k_specs = [pl.BlockSpec((1, k.shape[1], HEAD_DIM), lambda b, h, i: (b, 0, h)) for k in ks]
    v_specs = [pl.BlockSpec((1, v.shape[1], 2 * HEAD_DIM), lambda b, h, i: (b, 0, h)) for v in vs]
    return pl.pallas_call(
        functools.partial(_attn_kernel, nseg=nseg),
        grid=(b, N_KV, n // tq),
        in_specs=[pl.BlockSpec((1, tq, gw), lambda b, h, i: (b, i, h))] + k_specs + v_specs,
        out_specs=pl.BlockSpec((1, tq, gw), lambda b, h, i: (b, i, h)),
        out_shape=jax.ShapeDtypeStruct((b, n, N_HEADS * HEAD_DIM), bf16),
        compiler_params=_params(("arbitrary", "arbitrary", "arbitrary")),
        name="attn",
    )(q, *ks, *vs)


def _pad2(a, rows, cols):
    return jnp.pad(a, ((0, rows - a.shape[0]), (0, cols - a.shape[1])))


def kernel(x, c, ctx, c_ctx, mod_w, mod_b, norm1_g, norm2_g, ffn_w_gu, ffn_w_down, even_w_in,
           gmlp_ln_g, gmlp_ln_b, gmlp_w_s, gmlp_b_s, hyena_conv_w, hyena_conv_b, hyena_f_w1,
           hyena_f_b1, hyena_f_w2, hyena_f_b2, hyena_f_w3, hyena_freq, hyena_skip, even_w_out,
           attn_w_qkv, attn_q_g, attn_k_g, attn_w_o, final_g):
    lat_row = lambda b: b
    ctx_row = lambda b: CTX_ROW
    streams = {"lat": (SEQ, lat_row, 512), "ctx": (CTX, ctx_row, 512)}

    cond = jnp.zeros((MOD_ROWS, D), f32).at[:NB].set(c).at[CTX_ROW].set(c_ctx)
    mods = _mods(cond, mod_w, mod_b)

    dft = {p: _dft_consts(p) for p in set(HY_P.values())}
    fconst = {n: _filter_consts(n) for n in (SEQ, CTX)}
    rope = _rope_tables(SEQ)
    nq = N_HEADS * HEAD_DIM
    nk = N_KV * HEAD_DIM

    xs = {"lat": x, "ctx": ctx.reshape(1, NB * CTX, D)}
    for layer in range(DEPTH):
        last = layer == DEPTH - 1
        is_even = layer % 2 == 0
        mods3 = mods[layer].reshape(MOD_ROWS, 1, 6 * D)
        g1 = norm1_g[layer].reshape(1, D)
        g2 = norm2_g[layer].reshape(1, D)
        w_gu = ffn_w_gu[layer].astype(bf16)
        w_down = ffn_w_down[layer].astype(bf16)
        mixed = {}
        if is_even:
            i = layer // 2
            w_in = even_w_in[i].astype(bf16)
            w_out = even_w_out[i].astype(bf16)
            w_s = gmlp_w_s[i].astype(bf16)
            b_s_rows = jnp.repeat(gmlp_b_s[i].T, CHUNK, axis=1)
            ln_g = gmlp_ln_g[i].reshape(1, D)
            ln_b = gmlp_ln_b[i].reshape(1, D)
            conv_w3 = hyena_conv_w[i].reshape(3, 3, D).transpose(1, 0, 2)
            conv_b3 = hyena_conv_b[i].reshape(3, 1, D)
            w1p = _pad2(hyena_f_w1[i], LANES, LANES)
            w2p = _pad2(hyena_f_w2[i], LANES, LANES)
            b1p = _pad2(hyena_f_b1[i][None, :], 1, LANES)
            b2p = _pad2(hyena_f_b2[i][None, :], 1, LANES)
            f0p = _pad2(hyena_freq[i, 0][None, :], 1, LANES)
            f1p = _pad2(hyena_freq[i, 1][None, :], 1, LANES)
            w3p = jnp.pad(hyena_f_w3[i].reshape(FILTER_HIDDEN, 2, 2 * D).transpose(1, 0, 2),
                          ((0, 0), (0, LANES - FILTER_HIDDEN), (0, 0)))
            for key, (n, row_of, tm) in streams.items():
                if key == "ctx" and last:
                    continue
                xc = xs[key]
                p = HY_P[n]
                ffwd, finv, gmat = dft[p]
                zemb, deltas = fconst[n]
                hraw, hsum = _filter_time(n, zemb, deltas, w1p, b1p, w2p, b2p, f0p, f1p, w3p)
                h_a, h_b = _filter_spec(n, p, gmat, hraw, hsum)
                y_a, pb = _even_in(xc, g1, mods3, row_of, w_in, ln_g, ln_b, w_s, b_s_rows, tm)
                z_b = _hyena(pb.reshape(NB, n, 3 * D), conv_w3, conv_b3, hyena_skip[i],
                             h_a, h_b, ffwd, finv, p)
                mixed[key] = ([y_a, z_b.reshape(y_a.shape)], [w_out[:D], w_out[D:]])
        else:
            j = layer // 2
            w_qkv = attn_w_qkv[j].astype(bf16)
            w_o = attn_w_o[j].astype(bf16)
            q_g = attn_q_g[j].reshape(1, HEAD_DIM)
            k_g = attn_k_g[j].reshape(1, HEAD_DIM)
            q_l, k_l, v_l = _qkv(xs["lat"], g1, mods3, lat_row, w_qkv, q_g, k_g, rope, True, 512)
            if last:
                k_c, v_c = _qkv(xs["ctx"], g1, mods3, ctx_row, w_qkv[:, nq:], q_g, k_g, None, False, 512)
            else:
                q_c, k_c, v_c = _qkv(xs["ctx"], g1, mods3, ctx_row, w_qkv, q_g, k_g, None, True, 512)
            k_c = k_c.reshape(NB, CTX, nk)
            v_c = v_c.reshape(NB, CTX, 2 * nk)
            mixed["lat"] = ([_attention(q_l, [k_c, k_l], [v_c, v_l], 512)], [w_o])
            if not last:
                o_c = _attention(q_c.reshape(NB, CTX, nq), [k_c], [v_c], CTX)
                mixed["ctx"] = ([o_c.reshape(1, NB * CTX, nq)], [w_o])
        for key, (acts, ws) in mixed.items():
            _, row_of, tm = streams[key]
            fg = final_g.reshape(1, D) if last else None
            xs[key] = _mix_ffn(xs[key], mods3, row_of, acts, ws, g2, w_gu, w_down, tm, fg)
    return xs["lat"]
```

```python
import functools
import math

import numpy as np
import jax
import jax.numpy as jnp
from jax import lax
from jax.experimental import pallas as pl
from jax.experimental.pallas import tpu as pltpu

D = 1024
NB = 16
SEQ = 2048
CTX = 256
DEPTH = 4
GRID_W = 64
EPS = 1e-6
CHUNK = 128
A_GROUPS = 8
HEAD_DIM = 128
N_HEADS = 8
N_KV = 2
GQA = N_HEADS // N_KV
D_FF = 2816
FILTER_EMB = 33
FILTER_BANDS = 16
FILTER_HIDDEN = 64
MIN_DECAY = math.log(1e-2) / 1.5
MAX_DECAY = math.log(1e-2) / 0.3
ROPE_THETA = 10000.0

LANES = 128
CTX_ROW = NB
MOD_ROWS = 24
HY_P = {SEQ: 1024, CTX: 256}
HY_TC = LANES
HY_NBAT = 2
HY_ROWS = 32
FFN_TF = 256
VMEM_LIMIT = 56 * 1024 * 1024

bf16 = jnp.bfloat16
f32 = jnp.float32


_NT = (((1,), (1,)), ((), ()))


def _dot(a, b):
    return jnp.dot(a, b, preferred_element_type=f32)


def _mm_f32(a, b):
    return jnp.dot(a, b, preferred_element_type=f32, precision=lax.Precision.HIGHEST)


def _params(sem):
    return pltpu.CompilerParams(dimension_semantics=sem, vmem_limit_bytes=VMEM_LIMIT)


def _whole(a):
    nd = a.ndim
    return pl.BlockSpec(a.shape, lambda *_: (0,) * nd)


def _mods_kernel(cond_ref, w_ref, b_ref, o_ref):
    a = jax.nn.silu(cond_ref[...]).astype(bf16)
    o_ref[0] = _dot(a, w_ref[0].astype(bf16)) + b_ref[0]


def _mods(cond, mod_w, mod_b):
    tn = 1024
    n6 = 6 * D
    return pl.pallas_call(
        _mods_kernel,
        grid=(DEPTH, n6 // tn),
        in_specs=[
            pl.BlockSpec((MOD_ROWS, D), lambda l, j: (0, 0)),
            pl.BlockSpec((1, D, tn), lambda l, j: (l, 0, j)),
            pl.BlockSpec((1, 1, tn), lambda l, j: (l, 0, j)),
        ],
        out_specs=pl.BlockSpec((1, MOD_ROWS, tn), lambda l, j: (l, 0, j)),
        out_shape=jax.ShapeDtypeStruct((DEPTH, MOD_ROWS, n6), f32),
        compiler_params=_params(("arbitrary", "arbitrary")),
        name="mods",
    )(cond, mod_w, mod_b.reshape(DEPTH, 1, n6))


def _mod_spec(row_of, chunk):
    return pl.BlockSpec((1, 1, D), lambda b, i: (row_of(b), 0, chunk))


def _row_spec(tm, width):
    return pl.BlockSpec((1, tm, width), lambda b, i: (b, i, 0))


def _norm_mod(x, g, sc, sh):
    y = x * lax.rsqrt(jnp.mean(x * x, axis=-1, keepdims=True) + EPS)
    return (y * g) * (1.0 + sc) + sh


def _gelu(x):
    return 0.5 * x * (1.0 + lax.erf(x * (2.0 ** -0.5)))


def _even_in_kernel(x_ref, g_ref, sh_ref, sc_ref, w_ref, lng_ref, lnb_ref, ws_ref, bs_ref,
                    ya_ref, pb_ref, *, tm):
    h = _norm_mod(x_ref[0], g_ref[...], sc_ref[0], sh_ref[0]).astype(bf16)
    pb_ref[0] = _dot(h, w_ref[:, 2 * D:])
    u = _gelu(_dot(h, w_ref[:, :D]))
    v = _gelu(_dot(h, w_ref[:, D:2 * D]))
    mu = jnp.mean(v, axis=-1, keepdims=True)
    vc = v - mu
    var = jnp.mean(vc * vc, axis=-1, keepdims=True)
    vn = (vc * lax.rsqrt(var + 1e-5) * lng_ref[...] + lnb_ref[...]).astype(bf16)
    nck = tm // CHUNK
    for g in range(A_GROUPS):
        cols = slice(g * CHUNK, (g + 1) * CHUNK)
        rhs = jnp.concatenate([vn[k * CHUNK:(k + 1) * CHUNK, cols] for k in range(nck)], axis=1)
        sv = _dot(ws_ref[g], rhs)
        for k in range(nck):
            rows = slice(k * CHUNK, (k + 1) * CHUNK)
            gate = sv[:, k * CHUNK:(k + 1) * CHUNK] + bs_ref[:, cols]
            ya_ref[0, rows, cols] = (u[rows, cols] * gate).astype(bf16)


def _even_in(x, g, mods3, row_of, w_in, ln_g, ln_b, w_s, b_s_rows, tm):
    b, n, _ = x.shape
    return pl.pallas_call(
        functools.partial(_even_in_kernel, tm=tm),
        grid=(b, n // tm),
        in_specs=[
            _row_spec(tm, D), _whole(g), _mod_spec(row_of, 0), _mod_spec(row_of, 1),
            _whole(w_in), _whole(ln_g), _whole(ln_b), _whole(w_s), _whole(b_s_rows),
        ],
        out_specs=[_row_spec(tm, D), _row_spec(tm, 3 * D)],
        out_shape=[jax.ShapeDtypeStruct((b, n, D), bf16), jax.ShapeDtypeStruct((b, n, 3 * D), f32)],
        compiler_params=_params(("arbitrary", "arbitrary")),
        name="even_in",
    )(x, g, mods3, mods3, w_in, ln_g, ln_b, w_s, b_s_rows)


def _filter_consts(n):
    lag = np.abs(np.arange(2 * n) - n)
    lag[0] = 0
    t = np.linspace(0.0, 1.0, n)[lag][:, None]
    w = (2.0 * math.pi / n) * lag[:, None]
    bands = np.linspace(1e-4, FILTER_BANDS - 1, FILTER_BANDS)[None, :]
    ang = bands * w
    z = np.concatenate([t, np.cos(ang), -np.sin(ang)], axis=-1)
    zp = np.zeros((2 * n, LANES), np.float64)
    zp[:, :FILTER_EMB] = z
    deltas = np.abs(np.linspace(MIN_DECAY, MAX_DECAY, D))
    return jnp.asarray(zp, f32), jnp.asarray(np.tile(deltas, 2)[None, :], f32)


def _filter_time_kernel(z_ref, w1_ref, b1_ref, w2_ref, b2_ref, f0_ref, f1_ref, w3_ref, dl_ref,
                        h_ref, s_ref, *, tr):
    r = pl.program_id(0)
    z = z_ref[...]
    hid = jnp.sin(f0_ref[...] * (_mm_f32(z, w1_ref[...]) + b1_ref[...]))
    hid = jnp.sin(f1_ref[...] * (_mm_f32(hid, w2_ref[...]) + b2_ref[...]))
    h = _mm_f32(hid, w3_ref[0])
    h = h * jnp.exp(-z[:, 0:1] * dl_ref[...])
    row = lax.broadcasted_iota(jnp.int32, h.shape, 0) + r * tr
    h = jnp.where(row == 0, 0.0, h)
    h_ref[...] = h

    @pl.when(r == 0)
    def _():
        s_ref[...] = jnp.zeros_like(s_ref)

    s_ref[...] += jnp.sum(jnp.abs(h), axis=0, keepdims=True)


def _filter_time(n, zemb, deltas, w1p, b1p, w2p, b2p, f0p, f1p, w3p):
    tr = 256
    half = n // tr
    c2 = 2 * D
    return pl.pallas_call(
        functools.partial(_filter_time_kernel, tr=tr),
        grid=(2 * n // tr,),
        in_specs=[
            pl.BlockSpec((tr, LANES), lambda r: (r, 0)),
            _whole(w1p), _whole(b1p), _whole(w2p), _whole(b2p), _whole(f0p), _whole(f1p),
            pl.BlockSpec((1, LANES, c2), lambda r: (jnp.where(r < half, 1, 0), 0, 0)),
            _whole(deltas),
        ],
        out_specs=[pl.BlockSpec((tr, c2), lambda r: (r, 0)), pl.BlockSpec((1, c2), lambda r: (0, 0))],
        out_shape=[jax.ShapeDtypeStruct((2 * n, c2), f32), jax.ShapeDtypeStruct((1, c2), f32)],
        compiler_params=_params(("arbitrary",)),
        name="filter_time",
    )(zemb, w1p, b1p, w2p, b2p, f0p, f1p, w3p, deltas)


def _split_bf16(a):
    hi = a.astype(bf16)
    return hi, (a - hi.astype(f32)).astype(bf16)


def _dft_consts(p):
    q = p // 2
    theta = np.pi * (2 * np.arange(q) + 1) / (2 * p)
    te = theta[:, None] * (2 * np.arange(q))[None, :]
    to = theta[:, None] * (2 * np.arange(q) + 1)[None, :]
    fe = np.concatenate([np.cos(te), -np.sin(te)], axis=0)
    fo = np.concatenate([np.cos(to), -np.sin(to)], axis=0)
    ie = np.concatenate([np.cos(te).T, -np.sin(te).T], axis=1) / p
    io = np.concatenate([np.cos(to).T, -np.sin(to).T], axis=1) / p
    freq = np.concatenate([np.arange(q), p - 1 - np.arange(q)])
    tf = (np.pi * (2 * freq + 1) / (2 * p))[:, None] * np.arange(-p, p)[None, :]
    g = np.concatenate([np.cos(tf), -np.sin(tf)], axis=0)
    g[:, 0] = 0.0
    return tuple(jnp.asarray(a, bf16) for a in (fe, fo, ie, io, g))


def _filter_spec_kernel(g_ref, lo_ref, hi_ref, s_ref, ha_ref, hb_ref, *, p):
    h = (_dot(g_ref[:, :p], lo_ref[...].astype(bf16))
         + _dot(g_ref[:, p:], hi_ref[...].astype(bf16)))
    h = h / s_ref[...]
    ha_ref[0] = h[:p]
    hb_ref[0] = h[p:]


def _filter_spec(n, p, gmat, hraw, hsum):
    nb = n // p
    c2 = 2 * D
    tc = 512
    nd = 2 * nb - 1
    out = jax.ShapeDtypeStruct((nd, p, c2), f32)
    return pl.pallas_call(
        functools.partial(_filter_spec_kernel, p=p),
        grid=(nd, c2 // tc),
        in_specs=[
            _whole(gmat),
            pl.BlockSpec((p, tc), lambda d, c: (d, c)),
            pl.BlockSpec((p, tc), lambda d, c: (d + 1, c)),
            pl.BlockSpec((1, tc), lambda d, c: (0, c)),
        ],
        out_specs=[pl.BlockSpec((1, p, tc), lambda d, c: (d, 0, c))] * 2,
        out_shape=[out, out],
        compiler_params=_params(("arbitrary", "arbitrary")),
        name="filter_spec",
    )(gmat, hraw, hraw, hsum)


def _hyena_kernel(pv_ref, p1_ref, p2_ref, cw_ref, cb_ref, skip_ref,
                  ha0_ref, hb0_ref, ha1_ref, hb1_ref, fe_ref, fo_ref, ie_ref, io_ref,
                  o_ref, sv_ref, s1_ref, s2_ref, u_ref, za_ref, zb_ref, ye_ref, yo_ref, *, n, p):
    nb = n // p
    q = p // 2
    tc = HY_TC
    halves = [slice(bb * tc, (bb + 1) * tc) for bb in range(HY_NBAT)]

    def short_conv(p_ref, part, dst_ref):
        w = cw_ref[part]
        w0, w1, w2, bias = w[0:1], w[1:2], w[2:3], cb_ref[part]
        edge = lax.broadcasted_iota(jnp.int32, (8, tc), 0)
        for bb in range(HY_NBAT):
            dst_ref[bb, 8:n - 8, :] = (p_ref[bb, 7:n - 9, :] * w0 + p_ref[bb, 8:n - 8, :] * w1
                                       + p_ref[bb, 9:n - 7, :] * w2 + bias)
            top = p_ref[bb, 0:8, :]
            prev = jnp.where(edge == 0, 0.0, pltpu.roll(top, 1, axis=0))
            dst_ref[bb, 0:8, :] = prev * w0 + top * w1 + p_ref[bb, 1:9, :] * w2 + bias
            bot = p_ref[bb, n - 8:n, :]
            nxt = jnp.where(edge == 7, 0.0, pltpu.roll(bot, 7, axis=0))
            dst_ref[bb, n - 8:n, :] = p_ref[bb, n - 9:n - 1, :] * w0 + bot * w1 + nxt * w2 + bias

    short_conv(pv_ref, 0, sv_ref)
    short_conv(p1_ref, 1, s1_ref)
    short_conv(p2_ref, 2, s2_ref)

    def samples(ref, start):
        rows = pl.ds(start, q, stride=2)
        return jnp.concatenate([ref.at[bb][rows, :] for bb in range(HY_NBAT)], axis=1)

    def long_conv(src_ref, gate_ref, ha_ref, hb_ref, skip, dst_ref):
        for j in range(nb):
            e = _dot(fe_ref[...], samples(src_ref, j * p).astype(bf16))
            o = _dot(fo_ref[...], samples(src_ref, j * p + 1).astype(bf16))
            za_ref[j, 0:q] = e[:q] + o[:q]
            za_ref[j, q:p] = e[:q] - o[:q]
            zb_ref[j, 0:q] = e[q:] + o[q:]
            zb_ref[j, q:p] = o[q:] - e[q:]
        for i in range(nb):
            for c in range(q // HY_ROWS):
                lo = slice(c * HY_ROWS, (c + 1) * HY_ROWS)
                hi = slice(q + c * HY_ROWS, q + (c + 1) * HY_ROWS)
                acc = []
                for rows in (lo, hi):
                    ya = [None] * HY_NBAT
                    yb = [None] * HY_NBAT
                    for j in range(nb):
                        d = i - j + (nb - 1)
                        h_a = ha_ref[d, rows, :]
                        h_b = hb_ref[d, rows, :]
                        for bb, lanes in enumerate(halves):
                            z_a = za_ref[j, rows, lanes]
                            z_b = zb_ref[j, rows, lanes]
                            t_a = h_a * z_a - h_b * z_b
                            t_b = h_a * z_b + h_b * z_a
                            ya[bb] = t_a if ya[bb] is None else ya[bb] + t_a
                            yb[bb] = t_b if yb[bb] is None else yb[bb] + t_b
                    acc.append((ya, yb))
                (ya_lo, yb_lo), (ya_hi, yb_hi) = acc
                for bb, lanes in enumerate(halves):
                    ye_ref[i, lo, lanes] = (ya_lo[bb] + ya_hi[bb]).astype(bf16)
                    ye_ref[i, hi, lanes] = (yb_lo[bb] - yb_hi[bb]).astype(bf16)
                    yo_ref[i, lo, lanes] = (ya_lo[bb] - ya_hi[bb]).astype(bf16)
                    yo_ref[i, hi, lanes] = (yb_lo[bb] + yb_hi[bb]).astype(bf16)
            for parity, y in ((0, _dot(ie_ref[...], ye_ref[i])), (1, _dot(io_ref[...], yo_ref[i]))):
                rows = pl.ds(i * p + parity, q, stride=2)
                for bb, lanes in enumerate(halves):
                    dst_ref.at[bb][rows, :] = gate_ref.at[bb][rows, :] * (
                        y[:, lanes] + src_ref.at[bb][rows, :] * skip)

    long_conv(sv_ref, s1_ref, ha0_ref, hb0_ref, skip_ref[0:1, :], u_ref)
    long_conv(u_ref, s2_ref, ha1_ref, hb1_ref, skip_ref[1:2, :], sv_ref)
    for bb in range(HY_NBAT):
        o_ref[bb] = sv_ref[bb].astype(o_ref.dtype)


def _hyena(pb, conv_w3, conv_b3, skip, h_a, h_b, consts, p):
    b, n, _ = pb.shape
    tc = HY_TC
    nct = D // tc
    nb = n // p
    nd = 2 * nb - 1
    wide = HY_NBAT * tc

    def pspec(part):
        return pl.BlockSpec((HY_NBAT, n, tc), lambda c, b: (b, 0, part * nct + c))

    def hspec(order):
        return pl.BlockSpec((nd, p, tc), lambda c, b: (0, 0, order * nct + c),
                            pipeline_mode=pl.Buffered(1))

    return pl.pallas_call(
        functools.partial(_hyena_kernel, n=n, p=p),
        grid=(nct, b // HY_NBAT),
        in_specs=[
            pspec(0), pspec(1), pspec(2),
            pl.BlockSpec((3, 3, tc), lambda c, b: (0, 0, c)),
            pl.BlockSpec((3, 1, tc), lambda c, b: (0, 0, c)),
            pl.BlockSpec((2, tc), lambda c, b: (0, c)),
            hspec(0), hspec(0), hspec(1), hspec(1),
        ] + [_whole(a) for a in consts],
        out_specs=pl.BlockSpec((HY_NBAT, n, tc), lambda c, b: (b, 0, c)),
        out_shape=jax.ShapeDtypeStruct((b, n, D), bf16),
        scratch_shapes=(
            [pltpu.VMEM((HY_NBAT, n, tc), f32)] * 4
            + [pltpu.VMEM((nb, p, wide), f32)] * 2
            + [pltpu.VMEM((nb, p, wide), bf16)] * 2
        ),
        compiler_params=_params(("arbitrary", "arbitrary")),
        name="hyena",
    )(pb, pb, pb, conv_w3, conv_b3, skip, h_a, h_b, h_a, h_b, *consts)


def _mix_ffn_kernel(*refs, na, final):
    x_ref, gate1_ref = refs[:2]
    a_refs = refs[2:2 + na]
    wo_ref, g_ref, sh_ref, sc_ref, gate2_ref, wgu_ref, wd_ref = refs[2 + na:9 + na]
    rest = refs[9 + na:]
    fg_ref = rest[0] if final else None
    o_ref, x1_ref = rest[-2:]

    mix = _dot(a_refs[0][0], wo_ref[0:D, :])
    for k in range(1, na):
        mix = mix + _dot(a_refs[k][0], wo_ref[k * D:(k + 1) * D, :])
    x1_ref[...] = x_ref[0] + gate1_ref[0] * mix

    h = _norm_mod(x1_ref[...], g_ref[...], sc_ref[0], sh_ref[0]).astype(bf16)
    acc = None
    for f in range(D_FF // FFN_TF):
        cols = slice(f * FFN_TF, (f + 1) * FFN_TF)
        ucols = slice(D_FF + f * FFN_TF, D_FF + (f + 1) * FFN_TF)
        act = (jax.nn.silu(_dot(h, wgu_ref[:, cols])) * _dot(h, wgu_ref[:, ucols])).astype(bf16)
        part = _dot(act, wd_ref[cols, :])
        acc = part if acc is None else acc + part
    y = x1_ref[...] + gate2_ref[0] * acc
    if final:
        y = y * lax.rsqrt(jnp.mean(y * y, axis=-1, keepdims=True) + EPS) * fg_ref[...]
    o_ref[0] = y


def _mix_ffn(x, mods3, row_of, acts, w_o, g, w_gu, w_down, tm, final_g=None):
    b, n, _ = x.shape
    na = len(acts)
    assert w_o.shape == (na * D, D) and all(a.shape[-1] == D for a in acts)
    extra = [] if final_g is None else [final_g]
    return pl.pallas_call(
        functools.partial(_mix_ffn_kernel, na=na, final=final_g is not None),
        grid=(b, n // tm),
        in_specs=([_row_spec(tm, D), _mod_spec(row_of, 2)]
                  + [_row_spec(tm, D) for _ in acts]
                  + [_whole(w_o), _whole(g), _mod_spec(row_of, 3), _mod_spec(row_of, 4), _mod_spec(row_of, 5),
                     _whole(w_gu), _whole(w_down)] + [_whole(a) for a in extra]),
        out_specs=_row_spec(tm, D),
        out_shape=jax.ShapeDtypeStruct((b, n, D), f32),
        scratch_shapes=[pltpu.VMEM((tm, D), f32)],
        compiler_params=_params(("arbitrary", "arbitrary")),
        name="mix_ffn",
    )(x, mods3, *acts, w_o, g, mods3, mods3, mods3, w_gu, w_down, *extra)


def _rope_tables(n):
    rows = n // GRID_W
    row = np.repeat(np.arange(rows), GRID_W).astype(np.float64)
    col = np.tile(np.arange(GRID_W), rows).astype(np.float64)
    half = HEAD_DIM // 2
    inv = ROPE_THETA ** (-np.arange(0, half, 2, dtype=np.float64) / half)
    ang = np.concatenate([row[:, None] * inv, col[:, None] * inv], axis=-1)
    cos, sin = np.cos(ang), np.sin(ang)
    return (jnp.asarray(np.concatenate([cos, cos], axis=-1), f32),
            jnp.asarray(np.concatenate([-sin, sin], axis=-1), f32))


def _qkv_kernel(*refs, with_q, rope):
    x_ref, g_ref, sh_ref, sc_ref, w_ref, qg_ref, kg_ref = refs[:7]
    pos = 7
    if rope:
        cos_ref, sin_ref = refs[7:9]
        pos = 9
    out_refs = refs[pos:]
    h = _norm_mod(x_ref[0], g_ref[...], sc_ref[0], sh_ref[0]).astype(bf16)
    qkv = _dot(h, w_ref[...] if with_q else w_ref[:, N_HEADS * HEAD_DIM:])
    ones = jnp.ones((HEAD_DIM, HEAD_DIM), bf16)

    def head(t, gain, scale):
        sq_hi, sq_lo = _split_bf16(t * t)
        ssq = _dot(sq_hi, ones) + _dot(sq_lo, ones)
        t = t * lax.rsqrt(ssq * (1.0 / HEAD_DIM) + EPS) * gain
        if rope:
            t = t * cos_ref[...] + pltpu.roll(t, HEAD_DIM // 2, axis=1) * sin_ref[...]
        return (t * scale).astype(bf16)

    off = 0
    if with_q:
        q_ref, k_ref, v_ref = out_refs
        for hd in range(N_HEADS):
            cols = slice(hd * HEAD_DIM, (hd + 1) * HEAD_DIM)
            q_ref[0, :, cols] = head(qkv[:, cols], qg_ref[...], HEAD_DIM ** -0.5)
        off = N_HEADS * HEAD_DIM
    else:
        k_ref, v_ref = out_refs
    for kv in range(N_KV):
        cols = slice(kv * HEAD_DIM, (kv + 1) * HEAD_DIM)
        src = slice(off + kv * HEAD_DIM, off + (kv + 1) * HEAD_DIM)
        k_ref[0, :, cols] = head(qkv[:, src], kg_ref[...], 1.0)
    voff = off + N_KV * HEAD_DIM
    for kv in range(N_KV):
        src = slice(voff + kv * HEAD_DIM, voff + (kv + 1) * HEAD_DIM)
        v_ref[0, :, 2 * kv * HEAD_DIM:(2 * kv + 1) * HEAD_DIM] = qkv[:, src].astype(bf16)
        v_ref[0, :, (2 * kv + 1) * HEAD_DIM:(2 * kv + 2) * HEAD_DIM] = jnp.ones(
            (qkv.shape[0], HEAD_DIM), bf16)


def _qkv(x, g, mods3, row_of, w, q_g, k_g, tables, with_q, tm):
    b, n, _ = x.shape
    nq = N_HEADS * HEAD_DIM
    nk = N_KV * HEAD_DIM
    rope = tables is not None
    in_specs = [_row_spec(tm, D), _whole(g), _mod_spec(row_of, 0), _mod_spec(row_of, 1),
                _whole(w), _whole(q_g), _whole(k_g)]
    args = [x, g, mods3, mods3, w, q_g, k_g]
    if rope:
        in_specs += [pl.BlockSpec((tm, HEAD_DIM), lambda b, i: (i, 0))] * 2
        args += list(tables)
    out_specs = [_row_spec(tm, nk), _row_spec(tm, 2 * nk)]
    out_shape = [jax.ShapeDtypeStruct((b, n, nk), bf16), jax.ShapeDtypeStruct((b, n, 2 * nk), bf16)]
    if with_q:
        out_specs = [_row_spec(tm, nq)] + out_specs
        out_shape = [jax.ShapeDtypeStruct((b, n, nq), bf16)] + out_shape
    return pl.pallas_call(
        functools.partial(_qkv_kernel, with_q=with_q, rope=rope),
        grid=(b, n // tm),
        in_specs=in_specs,
        out_specs=out_specs,
        out_shape=out_shape,
        compiler_params=_params(("arbitrary", "arbitrary")),
        name="qkv",
    )(*args)


def _attn_kernel(*refs, nseg):
    q_ref = refs[0]
    k_refs = refs[1:1 + nseg]
    v_refs = refs[1 + nseg:1 + 2 * nseg]
    o_ref = refs[1 + 2 * nseg]
    def scores(hd):
        q = q_ref[0, :, hd * HEAD_DIM:(hd + 1) * HEAD_DIM]
        return [lax.dot_general(q, k[0], _NT, preferred_element_type=f32) for k in k_refs]

    s_next = scores(0)
    for hd in range(GQA):
        s = s_next
        if hd + 1 < GQA:
            s_next = scores(hd + 1)
        m = jnp.max(s[0], axis=-1, keepdims=True)
        for t in s[1:]:
            m = jnp.maximum(m, jnp.max(t, axis=-1, keepdims=True))
        ov = _dot(jnp.exp(s[0] - m).astype(bf16), v_refs[0][0])
        for t, v in zip(s[1:], v_refs[1:]):
            ov = ov + _dot(jnp.exp(t - m).astype(bf16), v[0])
        o_ref[0, :, hd * HEAD_DIM:(hd + 1) * HEAD_DIM] = (
            ov[:, :HEAD_DIM] / ov[:, HEAD_DIM:]).astype(o_ref.dtype)


def _attention(q, ks, vs, tq):
    b, n, _ = q.shape
    nseg = len(ks)
    gw = GQA * HEAD_DIM
    k_specs = [pl.BlockSpec((1, k.shape[1], HEAD_DIM), lambda b, h, i: (b, 0, h)) for k in ks]
    v_specs = [pl.BlockSpec((1, v.shape[1], 2 * HEAD_DIM), lambda b, h, i: (b, 0, h)) for v in vs]
    return pl.pallas_call(
        functools.partial(_attn_kernel, nseg=nseg),
        grid=(b, N_KV, n // tq),
        in_specs=[pl.BlockSpec((1, tq, gw), lambda b, h, i: (b, i, h))] + k_specs + v_specs,
        out_specs=pl.BlockSpec((1, tq, gw), lambda b, h, i: (b, i, h)),
        out_shape=jax.ShapeDtypeStruct((b, n, N_HEADS * HEAD_DIM), bf16),
        compiler_params=_params(("arbitrary", "arbitrary", "arbitrary")),
        name="attn",
    )(q, *ks, *vs)


def _pad2(a, rows, cols):
    return jnp.pad(a, ((0, rows - a.shape[0]), (0, cols - a.shape[1])))


def kernel(x, c, ctx, c_ctx, mod_w, mod_b, norm1_g, norm2_g, ffn_w_gu, ffn_w_down, even_w_in,
           gmlp_ln_g, gmlp_ln_b, gmlp_w_s, gmlp_b_s, hyena_conv_w, hyena_conv_b, hyena_f_w1,
           hyena_f_b1, hyena_f_w2, hyena_f_b2, hyena_f_w3, hyena_freq, hyena_skip, even_w_out,
           attn_w_qkv, attn_q_g, attn_k_g, attn_w_o, final_g):
    lat_row = lambda b: b
    ctx_row = lambda b: CTX_ROW
    streams = {"lat": (SEQ, lat_row, 512), "ctx": (CTX, ctx_row, 512)}

    cond = jnp.zeros((MOD_ROWS, D), f32).at[:NB].set(c).at[CTX_ROW].set(c_ctx)
    mods = _mods(cond, mod_w, mod_b)

    dft = {p: _dft_consts(p) for p in set(HY_P.values())}
    fconst = {n: _filter_consts(n) for n in (SEQ, CTX)}
    rope = _rope_tables(SEQ)
    nq = N_HEADS * HEAD_DIM
    nk = N_KV * HEAD_DIM

    xs = {"lat": x, "ctx": ctx.reshape(1, NB * CTX, D)}
    for layer in range(DEPTH):
        last = layer == DEPTH - 1
        is_even = layer % 2 == 0
        mods3 = mods[layer].reshape(MOD_ROWS, 1, 6 * D)
        g1 = norm1_g[layer].reshape(1, D)
        g2 = norm2_g[layer].reshape(1, D)
        w_gu = ffn_w_gu[layer].astype(bf16)
        w_down = ffn_w_down[layer].astype(bf16)
        mixed = {}
        if is_even:
            i = layer // 2
            w_in = even_w_in[i].astype(bf16)
            w_out = even_w_out[i].astype(bf16)
            w_s = gmlp_w_s[i].astype(bf16)
            b_s_rows = jnp.repeat(gmlp_b_s[i].T, CHUNK, axis=1)
            ln_g = gmlp_ln_g[i].reshape(1, D)
            ln_b = gmlp_ln_b[i].reshape(1, D)
            conv_w3 = hyena_conv_w[i].reshape(3, 3, D).transpose(1, 0, 2)
            conv_b3 = hyena_conv_b[i].reshape(3, 1, D)
            w1p = _pad2(hyena_f_w1[i], LANES, LANES)
            w2p = _pad2(hyena_f_w2[i], LANES, LANES)
            b1p = _pad2(hyena_f_b1[i][None, :], 1, LANES)
            b2p = _pad2(hyena_f_b2[i][None, :], 1, LANES)
            f0p = _pad2(hyena_freq[i, 0][None, :], 1, LANES)
            f1p = _pad2(hyena_freq[i, 1][None, :], 1, LANES)
            w3p = jnp.pad(hyena_f_w3[i].reshape(FILTER_HIDDEN, 2, 2 * D).transpose(1, 0, 2),
                          ((0, 0), (0, LANES - FILTER_HIDDEN), (0, 0)))
            for key, (n, row_of, tm) in streams.items():
                if key == "ctx" and last:
                    continue
                xc = xs[key]
                p = HY_P[n]
                *hy_consts, gmat = dft[p]
                zemb, deltas = fconst[n]
                hraw, hsum = _filter_time(n, zemb, deltas, w1p, b1p, w2p, b2p, f0p, f1p, w3p)
                h_a, h_b = _filter_spec(n, p, gmat, hraw, hsum)
                y_a, pb = _even_in(xc, g1, mods3, row_of, w_in, ln_g, ln_b, w_s, b_s_rows, tm)
                z_b = _hyena(pb.reshape(NB, n, 3 * D), conv_w3, conv_b3, hyena_skip[i],
                             h_a, h_b, hy_consts, p)
                mixed[key] = ([y_a, z_b.reshape(y_a.shape)], w_out)
        else:
            j = layer // 2
            w_qkv = attn_w_qkv[j].astype(bf16)
            w_o = attn_w_o[j].astype(bf16)
            q_g = attn_q_g[j].reshape(1, HEAD_DIM)
            k_g = attn_k_g[j].reshape(1, HEAD_DIM)
            q_l, k_l, v_l = _qkv(xs["lat"], g1, mods3, lat_row, w_qkv, q_g, k_g, rope, True, 512)
            if last:
                k_c, v_c = _qkv(xs["ctx"], g1, mods3, ctx_row, w_qkv, q_g, k_g, None, False, 512)
            else:
                q_c, k_c, v_c = _qkv(xs["ctx"], g1, mods3, ctx_row, w_qkv, q_g, k_g, None, True, 512)
            k_c = k_c.reshape(NB, CTX, nk)
            v_c = v_c.reshape(NB, CTX, 2 * nk)
            mixed["lat"] = ([_attention(q_l, [k_c, k_l], [v_c, v_l], 512)], w_o)
            if not last:
                o_c = _attention(q_c.reshape(NB, CTX, nq), [k_c], [v_c], CTX)
                mixed["ctx"] = ([o_c.reshape(1, NB * CTX, nq)], w_o)
        for key, (acts, w_mix) in mixed.items():
            _, row_of, tm = streams[key]
            fg = final_g.reshape(1, D) if last else None
            xs[key] = _mix_ffn(xs[key], mods3, row_of, acts, w_mix, g2, w_gu, w_down, tm, fg)
    return xs["lat"]
```

```python
import functools
import math

import numpy as np
import jax
import jax.numpy as jnp
from jax import lax
from jax.experimental import pallas as pl
from jax.experimental.pallas import tpu as pltpu

D = 1024
NB = 16
SEQ = 2048
CTX = 256
DEPTH = 4
GRID_W = 64
EPS = 1e-6
CHUNK = 128
A_GROUPS = 8
HEAD_DIM = 128
N_HEADS = 8
N_KV = 2
GQA = N_HEADS // N_KV
D_FF = 2816
FILTER_EMB = 33
FILTER_BANDS = 16
FILTER_HIDDEN = 64
MIN_DECAY = math.log(1e-2) / 1.5
MAX_DECAY = math.log(1e-2) / 0.3
ROPE_THETA = 10000.0

LANES = 128
CTX_ROW = NB
MOD_ROWS = 24
HY_P = {SEQ: 1024, CTX: 256}
HY_TC = LANES
HY_NBAT = 2
HY_ROWS = 32
FFN_TF = 256
VMEM_LIMIT = 56 * 1024 * 1024

bf16 = jnp.bfloat16
f32 = jnp.float32


_NT = (((1,), (1,)), ((), ()))


def _dot(a, b):
    return jnp.dot(a, b, preferred_element_type=f32)


def _mm_f32(a, b):
    return jnp.dot(a, b, preferred_element_type=f32, precision=lax.Precision.HIGHEST)


def _params(sem):
    return pltpu.CompilerParams(dimension_semantics=sem, vmem_limit_bytes=VMEM_LIMIT)


def _whole(a):
    nd = a.ndim
    return pl.BlockSpec(a.shape, lambda *_: (0,) * nd)


def _mods_kernel(cond_ref, w_ref, b_ref, o_ref):
    a = jax.nn.silu(cond_ref[...]).astype(bf16)
    o_ref[0] = _dot(a, w_ref[0].astype(bf16)) + b_ref[0]


def _mods(cond, mod_w, mod_b):
    tn = 1024
    n6 = 6 * D
    return pl.pallas_call(
        _mods_kernel,
        grid=(DEPTH, n6 // tn),
        in_specs=[
            pl.BlockSpec((MOD_ROWS, D), lambda l, j: (0, 0)),
            pl.BlockSpec((1, D, tn), lambda l, j: (l, 0, j)),
            pl.BlockSpec((1, 1, tn), lambda l, j: (l, 0, j)),
        ],
        out_specs=pl.BlockSpec((1, MOD_ROWS, tn), lambda l, j: (l, 0, j)),
        out_shape=jax.ShapeDtypeStruct((DEPTH, MOD_ROWS, n6), f32),
        compiler_params=_params(("arbitrary", "arbitrary")),
        name="mods",
    )(cond, mod_w, mod_b.reshape(DEPTH, 1, n6))


def _mod_spec(row_of, chunk):
    return pl.BlockSpec((1, 1, D), lambda b, i: (row_of(b), 0, chunk))


def _row_spec(tm, width):
    return pl.BlockSpec((1, tm, width), lambda b, i: (b, i, 0))


def _norm_mod(x, g, sc, sh):
    y = x * lax.rsqrt(jnp.mean(x * x, axis=-1, keepdims=True) + EPS)
    return (y * g) * (1.0 + sc) + sh


def _gelu(x):
    return 0.5 * x * (1.0 + lax.erf(x * (2.0 ** -0.5)))


def _even_in_kernel(x_ref, g_ref, sh_ref, sc_ref, w_ref, lng_ref, lnb_ref, ws_ref, bs_ref,
                    ya_ref, pb_ref, *, tm):
    h = _norm_mod(x_ref[0], g_ref[...], sc_ref[0], sh_ref[0]).astype(bf16)
    v = _gelu(_dot(h, w_ref[:, D:2 * D]))
    u = _gelu(_dot(h, w_ref[:, :D]))
    pb_ref[0] = _dot(h, w_ref[:, 2 * D:])
    mu = jnp.mean(v, axis=-1, keepdims=True)
    vc = v - mu
    var = jnp.mean(vc * vc, axis=-1, keepdims=True)
    vn = (vc * lax.rsqrt(var + 1e-5) * lng_ref[...] + lnb_ref[...]).astype(bf16)
    nck = tm // CHUNK
    for g in range(A_GROUPS):
        cols = slice(g * CHUNK, (g + 1) * CHUNK)
        rhs = jnp.concatenate([vn[k * CHUNK:(k + 1) * CHUNK, cols] for k in range(nck)], axis=1)
        sv = _dot(ws_ref[g], rhs)
        for k in range(nck):
            rows = slice(k * CHUNK, (k + 1) * CHUNK)
            gate = sv[:, k * CHUNK:(k + 1) * CHUNK] + bs_ref[:, cols]
            ya_ref[0, rows, cols] = (u[rows, cols] * gate).astype(bf16)


def _even_in(x, g, mods3, row_of, w_in, ln_g, ln_b, w_s, b_s_rows, tm):
    b, n, _ = x.shape
    return pl.pallas_call(
        functools.partial(_even_in_kernel, tm=tm),
        grid=(b, n // tm),
        in_specs=[
            _row_spec(tm, D), _whole(g), _mod_spec(row_of, 0), _mod_spec(row_of, 1),
            _whole(w_in), _whole(ln_g), _whole(ln_b), _whole(w_s), _whole(b_s_rows),
        ],
        out_specs=[_row_spec(tm, D), _row_spec(tm, 3 * D)],
        out_shape=[jax.ShapeDtypeStruct((b, n, D), bf16), jax.ShapeDtypeStruct((b, n, 3 * D), f32)],
        compiler_params=_params(("arbitrary", "arbitrary")),
        name="even_in",
    )(x, g, mods3, mods3, w_in, ln_g, ln_b, w_s, b_s_rows)


def _filter_consts(n):
    lag = np.abs(np.arange(2 * n) - n)
    lag[0] = 0
    t = np.linspace(0.0, 1.0, n)[lag][:, None]
    w = (2.0 * math.pi / n) * lag[:, None]
    bands = np.linspace(1e-4, FILTER_BANDS - 1, FILTER_BANDS)[None, :]
    ang = bands * w
    z = np.concatenate([t, np.cos(ang), -np.sin(ang)], axis=-1)
    zp = np.zeros((2 * n, LANES), np.float64)
    zp[:, :FILTER_EMB] = z
    deltas = np.abs(np.linspace(MIN_DECAY, MAX_DECAY, D))
    return jnp.asarray(zp, f32), jnp.asarray(np.tile(deltas, 2)[None, :], f32)


def _filter_time_kernel(z_ref, w1_ref, b1_ref, w2_ref, b2_ref, f0_ref, f1_ref, w3_ref, dl_ref,
                        h_ref, s_ref, *, tr):
    r = pl.program_id(0)
    z = z_ref[...]
    hid = jnp.sin(f0_ref[...] * (_mm_f32(z, w1_ref[...]) + b1_ref[...]))
    hid = jnp.sin(f1_ref[...] * (_mm_f32(hid, w2_ref[...]) + b2_ref[...]))
    h = _mm_f32(hid, w3_ref[0])
    h = h * jnp.exp(-z[:, 0:1] * dl_ref[...])
    row = lax.broadcasted_iota(jnp.int32, h.shape, 0) + r * tr
    h = jnp.where(row == 0, 0.0, h)
    h_ref[...] = h

    @pl.when(r == 0)
    def _():
        s_ref[...] = jnp.zeros_like(s_ref)

    s_ref[...] += jnp.sum(jnp.abs(h), axis=0, keepdims=True)


def _filter_time(n, zemb, deltas, w1p, b1p, w2p, b2p, f0p, f1p, w3p):
    tr = 256
    half = n // tr
    c2 = 2 * D
    return pl.pallas_call(
        functools.partial(_filter_time_kernel, tr=tr),
        grid=(2 * n // tr,),
        in_specs=[
            pl.BlockSpec((tr, LANES), lambda r: (r, 0)),
            _whole(w1p), _whole(b1p), _whole(w2p), _whole(b2p), _whole(f0p), _whole(f1p),
            pl.BlockSpec((1, LANES, c2), lambda r: (jnp.where(r < half, 1, 0), 0, 0)),
            _whole(deltas),
        ],
        out_specs=[pl.BlockSpec((tr, c2), lambda r: (r, 0)), pl.BlockSpec((1, c2), lambda r: (0, 0))],
        out_shape=[jax.ShapeDtypeStruct((2 * n, c2), f32), jax.ShapeDtypeStruct((1, c2), f32)],
        compiler_params=_params(("arbitrary",)),
        name="filter_time",
    )(zemb, w1p, b1p, w2p, b2p, f0p, f1p, w3p, deltas)


def _split_bf16(a):
    hi = a.astype(bf16)
    return hi, (a - hi.astype(f32)).astype(bf16)


def _dft_consts(p):
    q = p // 2
    theta = np.pi * (2 * np.arange(q) + 1) / (2 * p)
    te = theta[:, None] * (2 * np.arange(q))[None, :]
    to = theta[:, None] * (2 * np.arange(q) + 1)[None, :]
    fe = np.concatenate([np.cos(te), -np.sin(te)], axis=0)
    fo = np.concatenate([np.cos(to), -np.sin(to)], axis=0)
    ie = np.concatenate([np.cos(te).T, -np.sin(te).T], axis=1) / p
    io = np.concatenate([np.cos(to).T, -np.sin(to).T], axis=1) / p
    freq = np.concatenate([np.arange(q), p - 1 - np.arange(q)])
    tf = (np.pi * (2 * freq + 1) / (2 * p))[:, None] * np.arange(-p, p)[None, :]
    g = np.concatenate([np.cos(tf), -np.sin(tf)], axis=0)
    g[:, 0] = 0.0
    return tuple(jnp.asarray(a, bf16) for a in (fe, fo, ie, io, g))


def _filter_spec_kernel(g_ref, lo_ref, hi_ref, s_ref, ha_ref, hb_ref, *, p):
    h = (_dot(g_ref[:, :p], lo_ref[...].astype(bf16))
         + _dot(g_ref[:, p:], hi_ref[...].astype(bf16)))
    h = h / s_ref[...]
    ha_ref[0] = h[:p]
    hb_ref[0] = h[p:]


def _filter_spec(n, p, gmat, hraw, hsum):
    nb = n // p
    c2 = 2 * D
    tc = 512
    nd = 2 * nb - 1
    out = jax.ShapeDtypeStruct((nd, p, c2), f32)
    return pl.pallas_call(
        functools.partial(_filter_spec_kernel, p=p),
        grid=(nd, c2 // tc),
        in_specs=[
            _whole(gmat),
            pl.BlockSpec((p, tc), lambda d, c: (d, c)),
            pl.BlockSpec((p, tc), lambda d, c: (d + 1, c)),
            pl.BlockSpec((1, tc), lambda d, c: (0, c)),
        ],
        out_specs=[pl.BlockSpec((1, p, tc), lambda d, c: (d, 0, c))] * 2,
        out_shape=[out, out],
        compiler_params=_params(("arbitrary", "arbitrary")),
        name="filter_spec",
    )(gmat, hraw, hraw, hsum)


def _hyena_kernel(pv_ref, p1_ref, p2_ref, cw_ref, cb_ref, skip_ref,
                  ha0_ref, hb0_ref, ha1_ref, hb1_ref, fe_ref, fo_ref, ie_ref, io_ref,
                  o_ref, sv_ref, s1_ref, s2_ref, u_ref, za_ref, zb_ref, ye_ref, yo_ref, *, n, p):
    nb = n // p
    q = p // 2
    tc = HY_TC
    halves = [slice(bb * tc, (bb + 1) * tc) for bb in range(HY_NBAT)]

    def short_conv(p_ref, part, dst_ref):
        w = cw_ref[part]
        w0, w1, w2, bias = w[0:1], w[1:2], w[2:3], cb_ref[part]
        edge = lax.broadcasted_iota(jnp.int32, (8, tc), 0)
        for bb in range(HY_NBAT):
            dst_ref[bb, 8:n - 8, :] = (p_ref[bb, 7:n - 9, :] * w0 + p_ref[bb, 8:n - 8, :] * w1
                                       + p_ref[bb, 9:n - 7, :] * w2 + bias)
            top = p_ref[bb, 0:8, :]
            prev = jnp.where(edge == 0, 0.0, pltpu.roll(top, 1, axis=0))
            dst_ref[bb, 0:8, :] = prev * w0 + top * w1 + p_ref[bb, 1:9, :] * w2 + bias
            bot = p_ref[bb, n - 8:n, :]
            nxt = jnp.where(edge == 7, 0.0, pltpu.roll(bot, 7, axis=0))
            dst_ref[bb, n - 8:n, :] = p_ref[bb, n - 9:n - 1, :] * w0 + bot * w1 + nxt * w2 + bias

    def samples(ref, start):
        rows = pl.ds(start, q, stride=2)
        return jnp.concatenate([ref.at[bb][rows, :] for bb in range(HY_NBAT)], axis=1)

    def forward(src_ref):
        for j in range(nb):
            e = _dot(fe_ref[...], samples(src_ref, j * p).astype(bf16))
            o = _dot(fo_ref[...], samples(src_ref, j * p + 1).astype(bf16))
            za_ref[j, 0:q] = e[:q] + o[:q]
            za_ref[j, q:p] = e[:q] - o[:q]
            zb_ref[j, 0:q] = e[q:] + o[q:]
            zb_ref[j, q:p] = o[q:] - e[q:]

    def filter_and_gate(src_ref, gate_ref, ha_ref, hb_ref, skip, dst_ref):
        for i in range(nb):
            for c in range(q // HY_ROWS):
                lo = slice(c * HY_ROWS, (c + 1) * HY_ROWS)
                hi = slice(q + c * HY_ROWS, q + (c + 1) * HY_ROWS)
                acc = []
                for rows in (lo, hi):
                    ya = [None] * HY_NBAT
                    yb = [None] * HY_NBAT
                    for j in range(nb):
                        d = i - j + (nb - 1)
                        h_a = ha_ref[d, rows, :]
                        h_b = hb_ref[d, rows, :]
                        for bb, lanes in enumerate(halves):
                            z_a = za_ref[j, rows, lanes]
                            z_b = zb_ref[j, rows, lanes]
                            t_a = h_a * z_a - h_b * z_b
                            t_b = h_a * z_b + h_b * z_a
                            ya[bb] = t_a if ya[bb] is None else ya[bb] + t_a
                            yb[bb] = t_b if yb[bb] is None else yb[bb] + t_b
                    acc.append((ya, yb))
                (ya_lo, yb_lo), (ya_hi, yb_hi) = acc
                for bb, lanes in enumerate(halves):
                    ye_ref[i, lo, lanes] = (ya_lo[bb] + ya_hi[bb]).astype(bf16)
                    ye_ref[i, hi, lanes] = (yb_lo[bb] - yb_hi[bb]).astype(bf16)
                    yo_ref[i, lo, lanes] = (ya_lo[bb] - ya_hi[bb]).astype(bf16)
                    yo_ref[i, hi, lanes] = (yb_lo[bb] + yb_hi[bb]).astype(bf16)
            for parity, y in ((0, _dot(ie_ref[...], ye_ref[i])), (1, _dot(io_ref[...], yo_ref[i]))):
                rows = pl.ds(i * p + parity, q, stride=2)
                for bb, lanes in enumerate(halves):
                    dst_ref.at[bb][rows, :] = gate_ref.at[bb][rows, :] * (
                        y[:, lanes] + src_ref.at[bb][rows, :] * skip)

    short_conv(pv_ref, 0, sv_ref)
    forward(sv_ref)
    short_conv(p1_ref, 1, s1_ref)
    short_conv(p2_ref, 2, s2_ref)
    filter_and_gate(sv_ref, s1_ref, ha0_ref, hb0_ref, skip_ref[0:1, :], u_ref)
    forward(u_ref)
    filter_and_gate(u_ref, s2_ref, ha1_ref, hb1_ref, skip_ref[1:2, :], sv_ref)
    for bb in range(HY_NBAT):
        o_ref[bb] = sv_ref[bb].astype(o_ref.dtype)


def _hyena(pb, conv_w3, conv_b3, skip, h_a, h_b, consts, p):
    b, n, _ = pb.shape
    tc = HY_TC
    nct = D // tc
    nb = n // p
    nd = 2 * nb - 1
    wide = HY_NBAT * tc

    def pspec(part):
        return pl.BlockSpec((HY_NBAT, n, tc), lambda c, b: (b, 0, part * nct + c))

    def hspec(order):
        return pl.BlockSpec((nd, p, tc), lambda c, b: (0, 0, order * nct + c),
                            pipeline_mode=pl.Buffered(1))

    return pl.pallas_call(
        functools.partial(_hyena_kernel, n=n, p=p),
        grid=(nct, b // HY_NBAT),
        in_specs=[
            pspec(0), pspec(1), pspec(2),
            pl.BlockSpec((3, 3, tc), lambda c, b: (0, 0, c)),
            pl.BlockSpec((3, 1, tc), lambda c, b: (0, 0, c)),
            pl.BlockSpec((2, tc), lambda c, b: (0, c)),
            hspec(0), hspec(0), hspec(1), hspec(1),
        ] + [_whole(a) for a in consts],
        out_specs=pl.BlockSpec((HY_NBAT, n, tc), lambda c, b: (b, 0, c)),
        out_shape=jax.ShapeDtypeStruct((b, n, D), bf16),
        scratch_shapes=(
            [pltpu.VMEM((HY_NBAT, n, tc), f32)] * 4
            + [pltpu.VMEM((nb, p, wide), f32)] * 2
            + [pltpu.VMEM((nb, p, wide), bf16)] * 2
        ),
        compiler_params=_params(("arbitrary", "arbitrary")),
        name="hyena",
    )(pb, pb, pb, conv_w3, conv_b3, skip, h_a, h_b, h_a, h_b, *consts)


def _mix_ffn_kernel(*refs, na, final):
    x_ref, gate1_ref = refs[:2]
    a_refs = refs[2:2 + na]
    wo_ref, g_ref, sh_ref, sc_ref, gate2_ref, wgu_ref, wd_ref = refs[2 + na:9 + na]
    rest = refs[9 + na:]
    fg_ref = rest[0] if final else None
    o_ref, x1_ref = rest[-2:]

    mix = _dot(a_refs[0][0], wo_ref[0:D, :])
    for k in range(1, na):
        mix = mix + _dot(a_refs[k][0], wo_ref[k * D:(k + 1) * D, :])
    x1_ref[...] = x_ref[0] + gate1_ref[0] * mix

    h = _norm_mod(x1_ref[...], g_ref[...], sc_ref[0], sh_ref[0]).astype(bf16)
    acc = None
    for f in range(D_FF // FFN_TF):
        cols = slice(f * FFN_TF, (f + 1) * FFN_TF)
        ucols = slice(D_FF + f * FFN_TF, D_FF + (f + 1) * FFN_TF)
        act = (jax.nn.silu(_dot(h, wgu_ref[:, cols])) * _dot(h, wgu_ref[:, ucols])).astype(bf16)
        part = _dot(act, wd_ref[cols, :])
        acc = part if acc is None else acc + part
    y = x1_ref[...] + gate2_ref[0] * acc
    if final:
        y = y * lax.rsqrt(jnp.mean(y * y, axis=-1, keepdims=True) + EPS) * fg_ref[...]
    o_ref[0] = y


def _mix_ffn(x, mods3, row_of, acts, w_o, g, w_gu, w_down, tm, final_g=None):
    b, n, _ = x.shape
    na = len(acts)
    assert w_o.shape == (na * D, D) and all(a.shape[-1] == D for a in acts)
    extra = [] if final_g is None else [final_g]
    return pl.pallas_call(
        functools.partial(_mix_ffn_kernel, na=na, final=final_g is not None),
        grid=(b, n // tm),
        in_specs=([_row_spec(tm, D), _mod_spec(row_of, 2)]
                  + [_row_spec(tm, D) for _ in acts]
                  + [_whole(w_o), _whole(g), _mod_spec(row_of, 3), _mod_spec(row_of, 4), _mod_spec(row_of, 5),
                     _whole(w_gu), _whole(w_down)] + [_whole(a) for a in extra]),
        out_specs=_row_spec(tm, D),
        out_shape=jax.ShapeDtypeStruct((b, n, D), f32),
        scratch_shapes=[pltpu.VMEM((tm, D), f32)],
        compiler_params=_params(("arbitrary", "arbitrary")),
        name="mix_ffn",
    )(x, mods3, *acts, w_o, g, mods3, mods3, mods3, w_gu, w_down, *extra)


def _rope_tables(n):
    rows = n // GRID_W
    row = np.repeat(np.arange(rows), GRID_W).astype(np.float64)
    col = np.tile(np.arange(GRID_W), rows).astype(np.float64)
    half = HEAD_DIM // 2
    inv = ROPE_THETA ** (-np.arange(0, half, 2, dtype=np.float64) / half)
    ang = np.concatenate([row[:, None] * inv, col[:, None] * inv], axis=-1)
    cos, sin = np.cos(ang), np.sin(ang)
    return (jnp.asarray(np.concatenate([cos, cos], axis=-1), f32),
            jnp.asarray(np.concatenate([-sin, sin], axis=-1), f32))


def _qkv_kernel(*refs, with_q, rope):
    x_ref, g_ref, sh_ref, sc_ref, w_ref, qg_ref, kg_ref = refs[:7]
    pos = 7
    if rope:
        cos_ref, sin_ref = refs[7:9]
        pos = 9
    out_refs = refs[pos:]
    h = _norm_mod(x_ref[0], g_ref[...], sc_ref[0], sh_ref[0]).astype(bf16)
    qkv = _dot(h, w_ref[...] if with_q else w_ref[:, N_HEADS * HEAD_DIM:])
    ones = jnp.ones((HEAD_DIM, HEAD_DIM), bf16)

    def head(t, gain, scale):
        sq_hi, sq_lo = _split_bf16(t * t)
        ssq = _dot(sq_hi, ones) + _dot(sq_lo, ones)
        t = t * lax.rsqrt(ssq * (1.0 / HEAD_DIM) + EPS) * gain
        if rope:
            t = t * cos_ref[...] + pltpu.roll(t, HEAD_DIM // 2, axis=1) * sin_ref[...]
        return (t * scale).astype(bf16)

    off = 0
    if with_q:
        q_ref, k_ref, v_ref = out_refs
        for hd in range(N_HEADS):
            cols = slice(hd * HEAD_DIM, (hd + 1) * HEAD_DIM)
            q_ref[0, :, cols] = head(qkv[:, cols], qg_ref[...], HEAD_DIM ** -0.5)
        off = N_HEADS * HEAD_DIM
    else:
        k_ref, v_ref = out_refs
    for kv in range(N_KV):
        cols = slice(kv * HEAD_DIM, (kv + 1) * HEAD_DIM)
        src = slice(off + kv * HEAD_DIM, off + (kv + 1) * HEAD_DIM)
        k_ref[0, :, cols] = head(qkv[:, src], kg_ref[...], 1.0)
    voff = off + N_KV * HEAD_DIM
    for kv in range(N_KV):
        src = slice(voff + kv * HEAD_DIM, voff + (kv + 1) * HEAD_DIM)
        v_ref[0, :, 2 * kv * HEAD_DIM:(2 * kv + 1) * HEAD_DIM] = qkv[:, src].astype(bf16)
        v_ref[0, :, (2 * kv + 1) * HEAD_DIM:(2 * kv + 2) * HEAD_DIM] = jnp.ones(
            (qkv.shape[0], HEAD_DIM), bf16)


def _qkv(x, g, mods3, row_of, w, q_g, k_g, tables, with_q, tm):
    b, n, _ = x.shape
    nq = N_HEADS * HEAD_DIM
    nk = N_KV * HEAD_DIM
    rope = tables is not None
    in_specs = [_row_spec(tm, D), _whole(g), _mod_spec(row_of, 0), _mod_spec(row_of, 1),
                _whole(w), _whole(q_g), _whole(k_g)]
    args = [x, g, mods3, mods3, w, q_g, k_g]
    if rope:
        in_specs += [pl.BlockSpec((tm, HEAD_DIM), lambda b, i: (i, 0))] * 2
        args += list(tables)
    out_specs = [_row_spec(tm, nk), _row_spec(tm, 2 * nk)]
    out_shape = [jax.ShapeDtypeStruct((b, n, nk), bf16), jax.ShapeDtypeStruct((b, n, 2 * nk), bf16)]
    if with_q:
        out_specs = [_row_spec(tm, nq)] + out_specs
        out_shape = [jax.ShapeDtypeStruct((b, n, nq), bf16)] + out_shape
    return pl.pallas_call(
        functools.partial(_qkv_kernel, with_q=with_q, rope=rope),
        grid=(b, n // tm),
        in_specs=in_specs,
        out_specs=out_specs,
        out_shape=out_shape,
        compiler_params=_params(("arbitrary", "arbitrary")),
        name="qkv",
    )(*args)


def _attn_kernel(*refs, nseg):
    q_ref = refs[0]
    k_refs = refs[1:1 + nseg]
    v_refs = refs[1 + nseg:1 + 2 * nseg]
    o_ref = refs[1 + 2 * nseg]
    def scores(hd):
        q = q_ref[0, :, hd * HEAD_DIM:(hd + 1) * HEAD_DIM]
        return [lax.dot_general(q, k[0], _NT, preferred_element_type=f32) for k in k_refs]

    s_next = scores(0)
    for hd in range(GQA):
        s = s_next
        if hd + 1 < GQA:
            s_next = scores(hd + 1)
        m = jnp.max(s[0], axis=-1, keepdims=True)
        for t in s[1:]:
            m = jnp.maximum(m, jnp.max(t, axis=-1, keepdims=True))
        ov = _dot(jnp.exp(s[0] - m).astype(bf16), v_refs[0][0])
        for t, v in zip(s[1:], v_refs[1:]):
            ov = ov + _dot(jnp.exp(t - m).astype(bf16), v[0])
        o_ref[0, :, hd * HEAD_DIM:(hd + 1) * HEAD_DIM] = (
            ov[:, :HEAD_DIM] / ov[:, HEAD_DIM:]).astype(o_ref.dtype)


def _attention(q, ks, vs, tq):
    b, n, _ = q.shape
    nseg = len(ks)
    gw = GQA * HEAD_DIM
    k_specs = [pl.BlockSpec((1, k.shape[1], HEAD_DIM), lambda b, h, i: (b, 0, h)) for k in ks]
    v_specs = [pl.BlockSpec((1, v.shape[1], 2 * HEAD_DIM), lambda b, h, i: (b, 0, h)) for v in vs]
    return pl.pallas_call(
        functools.partial(_attn_kernel, nseg=nseg),
        grid=(b, N_KV, n // tq),
        in_specs=[pl.BlockSpec((1, tq, gw), lambda b, h, i: (b, i, h))] + k_specs + v_specs,
        out_specs=pl.BlockSpec((1, tq, gw), lambda b, h, i: (b, i, h)),
        out_shape=jax.ShapeDtypeStruct((b, n, N_HEADS * HEAD_DIM), bf16),
        compiler_params=_params(("arbitrary", "arbitrary", "arbitrary")),
        name="attn",
    )(q, *ks, *vs)


def _pad2(a, rows, cols):
    return jnp.pad(a, ((0, rows - a.shape[0]), (0, cols - a.shape[1])))


def kernel(x, c, ctx, c_ctx, mod_w, mod_b, norm1_g, norm2_g, ffn_w_gu, ffn_w_down, even_w_in,
           gmlp_ln_g, gmlp_ln_b, gmlp_w_s, gmlp_b_s, hyena_conv_w, hyena_conv_b, hyena_f_w1,
           hyena_f_b1, hyena_f_w2, hyena_f_b2, hyena_f_w3, hyena_freq, hyena_skip, even_w_out,
           attn_w_qkv, attn_q_g, attn_k_g, attn_w_o, final_g):
    lat_row = lambda b: b
    ctx_row = lambda b: CTX_ROW
    streams = {"lat": (SEQ, lat_row, 512), "ctx": (CTX, ctx_row, 512)}

    cond = jnp.zeros((MOD_ROWS, D), f32).at[:NB].set(c).at[CTX_ROW].set(c_ctx)
    mods = _mods(cond, mod_w, mod_b)

    dft = {p: _dft_consts(p) for p in set(HY_P.values())}
    fconst = {n: _filter_consts(n) for n in (SEQ, CTX)}
    rope = _rope_tables(SEQ)
    nq = N_HEADS * HEAD_DIM
    nk = N_KV * HEAD_DIM

    xs = {"lat": x, "ctx": ctx.reshape(1, NB * CTX, D)}
    for layer in range(DEPTH):
        last = layer == DEPTH - 1
        is_even = layer % 2 == 0
        mods3 = mods[layer].reshape(MOD_ROWS, 1, 6 * D)
        g1 = norm1_g[layer].reshape(1, D)
        g2 = norm2_g[layer].reshape(1, D)
        w_gu = ffn_w_gu[layer].astype(bf16)
        w_down = ffn_w_down[layer].astype(bf16)
        mixed = {}
        if is_even:
            i = layer // 2
            w_in = even_w_in[i].astype(bf16)
            w_out = even_w_out[i].astype(bf16)
            w_s = gmlp_w_s[i].astype(bf16)
            b_s_rows = jnp.repeat(gmlp_b_s[i].T, CHUNK, axis=1)
            ln_g = gmlp_ln_g[i].reshape(1, D)
            ln_b = gmlp_ln_b[i].reshape(1, D)
            conv_w3 = hyena_conv_w[i].reshape(3, 3, D).transpose(1, 0, 2)
            conv_b3 = hyena_conv_b[i].reshape(3, 1, D)
            w1p = _pad2(hyena_f_w1[i], LANES, LANES)
            w2p = _pad2(hyena_f_w2[i], LANES, LANES)
            b1p = _pad2(hyena_f_b1[i][None, :], 1, LANES)
            b2p = _pad2(hyena_f_b2[i][None, :], 1, LANES)
            f0p = _pad2(hyena_freq[i, 0][None, :], 1, LANES)
            f1p = _pad2(hyena_freq[i, 1][None, :], 1, LANES)
            w3p = jnp.pad(hyena_f_w3[i].reshape(FILTER_HIDDEN, 2, 2 * D).transpose(1, 0, 2),
                          ((0, 0), (0, LANES - FILTER_HIDDEN), (0, 0)))
            for key, (n, row_of, tm) in streams.items():
                if key == "ctx" and last:
                    continue
                xc = xs[key]
                p = HY_P[n]
                *hy_consts, gmat = dft[p]
                zemb, deltas = fconst[n]
                hraw, hsum = _filter_time(n, zemb, deltas, w1p, b1p, w2p, b2p, f0p, f1p, w3p)
                h_a, h_b = _filter_spec(n, p, gmat, hraw, hsum)
                y_a, pb = _even_in(xc, g1, mods3, row_of, w_in, ln_g, ln_b, w_s, b_s_rows, tm)
                z_b = _hyena(pb.reshape(NB, n, 3 * D), conv_w3, conv_b3, hyena_skip[i],
                             h_a, h_b, hy_consts, p)
                mixed[key] = ([y_a, z_b.reshape(y_a.shape)], w_out)
        else:
            j = layer // 2
            w_qkv = attn_w_qkv[j].astype(bf16)
            w_o = attn_w_o[j].astype(bf16)
            q_g = attn_q_g[j].reshape(1, HEAD_DIM)
            k_g = attn_k_g[j].reshape(1, HEAD_DIM)
            q_l, k_l, v_l = _qkv(xs["lat"], g1, mods3, lat_row, w_qkv, q_g, k_g, rope, True, 1024)
            if last:
                k_c, v_c = _qkv(xs["ctx"], g1, mods3, ctx_row, w_qkv, q_g, k_g, None, False, 512)
            else:
                q_c, k_c, v_c = _qkv(xs["ctx"], g1, mods3, ctx_row, w_qkv, q_g, k_g, None, True, 512)
            k_c = k_c.reshape(NB, CTX, nk)
            v_c = v_c.reshape(NB, CTX, 2 * nk)
            mixed["lat"] = ([_attention(q_l, [k_c, k_l], [v_c, v_l], 1024)], w_o)
            if not last:
                o_c = _attention(q_c.reshape(NB, CTX, nq), [k_c], [v_c], CTX)
                mixed["ctx"] = ([o_c.reshape(1, NB * CTX, nq)], w_o)
        for key, (acts, w_mix) in mixed.items():
            _, row_of, tm = streams[key]
            fg = final_g.reshape(1, D) if last else None
            xs[key] = _mix_ffn(xs[key], mods3, row_of, acts, w_mix, g2, w_gu, w_down, tm, fg)
    return xs["lat"]
```

```python
import functools
import math

import numpy as np
import jax
import jax.numpy as jnp
from jax import lax
from jax.experimental import pallas as pl
from jax.experimental.pallas import tpu as pltpu

D = 1024
NB = 16
SEQ = 2048
CTX = 256
DEPTH = 4
GRID_W = 64
EPS = 1e-6
CHUNK = 128
A_GROUPS = 8
HEAD_DIM = 128
N_HEADS = 8
N_KV = 2
GQA = N_HEADS // N_KV
D_FF = 2816
FILTER_EMB = 33
FILTER_BANDS = 16
FILTER_HIDDEN = 64
MIN_DECAY = math.log(1e-2) / 1.5
MAX_DECAY = math.log(1e-2) / 0.3
ROPE_THETA = 10000.0

LANES = 128
CTX_ROW = NB
MOD_ROWS = 24
HY_P = {SEQ: 1024, CTX: 256}
HY_TC = LANES
HY_NBAT = {SEQ: 2, CTX: 8}
HY_ROWS = 32
FFN_TF = 256
VMEM_LIMIT = 56 * 1024 * 1024

bf16 = jnp.bfloat16
f32 = jnp.float32


_NT = (((1,), (1,)), ((), ()))


def _dot(a, b):
    return jnp.dot(a, b, preferred_element_type=f32)


def _mm_f32(a, b):
    return jnp.dot(a, b, preferred_element_type=f32, precision=lax.Precision.HIGHEST)


def _params(sem):
    return pltpu.CompilerParams(dimension_semantics=sem, vmem_limit_bytes=VMEM_LIMIT)


def _whole(a):
    nd = a.ndim
    return pl.BlockSpec(a.shape, lambda *_: (0,) * nd)


def _mods_kernel(cond_ref, w_ref, b_ref, o_ref):
    a = jax.nn.silu(cond_ref[...]).astype(bf16)
    o_ref[0] = _dot(a, w_ref[0].astype(bf16)) + b_ref[0]


def _mods(cond, mod_w, mod_b):
    tn = 1024
    n6 = 6 * D
    return pl.pallas_call(
        _mods_kernel,
        grid=(DEPTH, n6 // tn),
        in_specs=[
            pl.BlockSpec((MOD_ROWS, D), lambda l, j: (0, 0)),
            pl.BlockSpec((1, D, tn), lambda l, j: (l, 0, j)),
            pl.BlockSpec((1, 1, tn), lambda l, j: (l, 0, j)),
        ],
        out_specs=pl.BlockSpec((1, MOD_ROWS, tn), lambda l, j: (l, 0, j)),
        out_shape=jax.ShapeDtypeStruct((DEPTH, MOD_ROWS, n6), f32),
        compiler_params=_params(("arbitrary", "arbitrary")),
        name="mods",
    )(cond, mod_w, mod_b.reshape(DEPTH, 1, n6))


def _mod_spec(row_of, chunk):
    return pl.BlockSpec((1, 1, D), lambda b, i: (row_of(b), 0, chunk))


def _row_spec(tm, width):
    return pl.BlockSpec((1, tm, width), lambda b, i: (b, i, 0))


def _norm_mod(x, g, sc, sh):
    y = x * lax.rsqrt(jnp.mean(x * x, axis=-1, keepdims=True) + EPS)
    return (y * g) * (1.0 + sc) + sh


def _gelu(x):
    return 0.5 * x * (1.0 + lax.erf(x * (2.0 ** -0.5)))


def _even_in_kernel(x_ref, g_ref, sh_ref, sc_ref, w_ref, lng_ref, lnb_ref, ws_ref, bs_ref,
                    ya_ref, pb_ref, *, tm):
    h = _norm_mod(x_ref[0], g_ref[...], sc_ref[0], sh_ref[0]).astype(bf16)
    v = _gelu(_dot(h, w_ref[:, D:2 * D]))
    u = _gelu(_dot(h, w_ref[:, :D]))
    pb_ref[0] = _dot(h, w_ref[:, 2 * D:])
    mu = jnp.mean(v, axis=-1, keepdims=True)
    vc = v - mu
    var = jnp.mean(vc * vc, axis=-1, keepdims=True)
    vn = (vc * lax.rsqrt(var + 1e-5) * lng_ref[...] + lnb_ref[...]).astype(bf16)
    nck = tm // CHUNK
    for g in range(A_GROUPS):
        cols = slice(g * CHUNK, (g + 1) * CHUNK)
        rhs = jnp.concatenate([vn[k * CHUNK:(k + 1) * CHUNK, cols] for k in range(nck)], axis=1)
        sv = _dot(ws_ref[g], rhs)
        for k in range(nck):
            rows = slice(k * CHUNK, (k + 1) * CHUNK)
            gate = sv[:, k * CHUNK:(k + 1) * CHUNK] + bs_ref[:, cols]
            ya_ref[0, rows, cols] = (u[rows, cols] * gate).astype(bf16)


def _even_in(x, g, mods3, row_of, w_in, ln_g, ln_b, w_s, b_s_rows, tm):
    b, n, _ = x.shape
    return pl.pallas_call(
        functools.partial(_even_in_kernel, tm=tm),
        grid=(b, n // tm),
        in_specs=[
            _row_spec(tm, D), _whole(g), _mod_spec(row_of, 0), _mod_spec(row_of, 1),
            _whole(w_in), _whole(ln_g), _whole(ln_b), _whole(w_s), _whole(b_s_rows),
        ],
        out_specs=[_row_spec(tm, D), _row_spec(tm, 3 * D)],
        out_shape=[jax.ShapeDtypeStruct((b, n, D), bf16), jax.ShapeDtypeStruct((b, n, 3 * D), f32)],
        compiler_params=_params(("arbitrary", "arbitrary")),
        name="even_in",
    )(x, g, mods3, mods3, w_in, ln_g, ln_b, w_s, b_s_rows)


def _filter_consts(n):
    lag = np.abs(np.arange(2 * n) - n)
    lag[0] = 0
    t = np.linspace(0.0, 1.0, n)[lag][:, None]
    w = (2.0 * math.pi / n) * lag[:, None]
    bands = np.linspace(1e-4, FILTER_BANDS - 1, FILTER_BANDS)[None, :]
    ang = bands * w
    z = np.concatenate([t, np.cos(ang), -np.sin(ang)], axis=-1)
    zp = np.zeros((2 * n, LANES), np.float64)
    zp[:, :FILTER_EMB] = z
    deltas = np.abs(np.linspace(MIN_DECAY, MAX_DECAY, D))
    return jnp.asarray(zp, f32), jnp.asarray(np.tile(deltas, 2)[None, :], f32)


def _filter_time_kernel(z_ref, w1_ref, b1_ref, w2_ref, b2_ref, f0_ref, f1_ref, w3_ref, dl_ref,
                        h_ref, s_ref, *, tr):
    r = pl.program_id(0)
    z = z_ref[...]
    hid = jnp.sin(f0_ref[...] * (_mm_f32(z, w1_ref[...]) + b1_ref[...]))
    hid = jnp.sin(f1_ref[...] * (_mm_f32(hid, w2_ref[...]) + b2_ref[...]))
    h = _mm_f32(hid, w3_ref[0])
    h = h * jnp.exp(-z[:, 0:1] * dl_ref[...])
    row = lax.broadcasted_iota(jnp.int32, h.shape, 0) + r * tr
    h = jnp.where(row == 0, 0.0, h)
    h_ref[...] = h

    @pl.when(r == 0)
    def _():
        s_ref[...] = jnp.zeros_like(s_ref)

    s_ref[...] += jnp.sum(jnp.abs(h), axis=0, keepdims=True)


def _filter_time(n, zemb, deltas, w1p, b1p, w2p, b2p, f0p, f1p, w3p):
    tr = 256
    half = n // tr
    c2 = 2 * D
    return pl.pallas_call(
        functools.partial(_filter_time_kernel, tr=tr),
        grid=(2 * n // tr,),
        in_specs=[
            pl.BlockSpec((tr, LANES), lambda r: (r, 0)),
            _whole(w1p), _whole(b1p), _whole(w2p), _whole(b2p), _whole(f0p), _whole(f1p),
            pl.BlockSpec((1, LANES, c2), lambda r: (jnp.where(r < half, 1, 0), 0, 0)),
            _whole(deltas),
        ],
        out_specs=[pl.BlockSpec((tr, c2), lambda r: (r, 0)), pl.BlockSpec((1, c2), lambda r: (0, 0))],
        out_shape=[jax.ShapeDtypeStruct((2 * n, c2), f32), jax.ShapeDtypeStruct((1, c2), f32)],
        compiler_params=_params(("arbitrary",)),
        name="filter_time",
    )(zemb, w1p, b1p, w2p, b2p, f0p, f1p, w3p, deltas)


def _split_bf16(a):
    hi = a.astype(bf16)
    return hi, (a - hi.astype(f32)).astype(bf16)


def _dft_consts(p):
    q = p // 2
    theta = np.pi * (2 * np.arange(q) + 1) / (2 * p)
    te = theta[:, None] * (2 * np.arange(q))[None, :]
    to = theta[:, None] * (2 * np.arange(q) + 1)[None, :]
    fe = np.concatenate([np.cos(te), -np.sin(te)], axis=0)
    fo = np.concatenate([np.cos(to), -np.sin(to)], axis=0)
    ie = np.concatenate([np.cos(te).T, -np.sin(te).T], axis=1) / p
    io = np.concatenate([np.cos(to).T, -np.sin(to).T], axis=1) / p
    freq = np.concatenate([np.arange(q), p - 1 - np.arange(q)])
    tf = (np.pi * (2 * freq + 1) / (2 * p))[:, None] * np.arange(-p, p)[None, :]
    g = np.concatenate([np.cos(tf), -np.sin(tf)], axis=0)
    g[:, 0] = 0.0
    return tuple(jnp.asarray(a, bf16) for a in (fe, fo, ie, io, g))


def _filter_spec_kernel(g_ref, lo_ref, hi_ref, s_ref, ha_ref, hb_ref, *, p):
    h = (_dot(g_ref[:, :p], lo_ref[...].astype(bf16))
         + _dot(g_ref[:, p:], hi_ref[...].astype(bf16)))
    h = h / s_ref[...]
    ha_ref[0] = h[:p]
    hb_ref[0] = h[p:]


def _filter_spec(n, p, gmat, hraw, hsum):
    nb = n // p
    c2 = 2 * D
    tc = 512
    nd = 2 * nb - 1
    out = jax.ShapeDtypeStruct((nd, p, c2), f32)
    return pl.pallas_call(
        functools.partial(_filter_spec_kernel, p=p),
        grid=(nd, c2 // tc),
        in_specs=[
            _whole(gmat),
            pl.BlockSpec((p, tc), lambda d, c: (d, c)),
            pl.BlockSpec((p, tc), lambda d, c: (d + 1, c)),
            pl.BlockSpec((1, tc), lambda d, c: (0, c)),
        ],
        out_specs=[pl.BlockSpec((1, p, tc), lambda d, c: (d, 0, c))] * 2,
        out_shape=[out, out],
        compiler_params=_params(("arbitrary", "arbitrary")),
        name="filter_spec",
    )(gmat, hraw, hraw, hsum)


def _hyena_kernel(pv_ref, p1_ref, p2_ref, cw_ref, cb_ref, skip_ref,
                  ha0_ref, hb0_ref, ha1_ref, hb1_ref, fe_ref, fo_ref, ie_ref, io_ref,
                  o_ref, sv_ref, s1_ref, s2_ref, u_ref, za_ref, zb_ref, ye_ref, yo_ref, *, n, p):
    nb = n // p
    q = p // 2
    tc = HY_TC
    nbat = o_ref.shape[0]
    halves = [slice(bb * tc, (bb + 1) * tc) for bb in range(nbat)]

    def short_conv(p_ref, part, dst_ref):
        w = cw_ref[part]
        w0, w1, w2, bias = w[0:1], w[1:2], w[2:3], cb_ref[part]
        edge = lax.broadcasted_iota(jnp.int32, (8, tc), 0)
        for bb in range(nbat):
            dst_ref[bb, 8:n - 8, :] = (p_ref[bb, 7:n - 9, :] * w0 + p_ref[bb, 8:n - 8, :] * w1
                                       + p_ref[bb, 9:n - 7, :] * w2 + bias)
            top = p_ref[bb, 0:8, :]
            prev = jnp.where(edge == 0, 0.0, pltpu.roll(top, 1, axis=0))
            dst_ref[bb, 0:8, :] = prev * w0 + top * w1 + p_ref[bb, 1:9, :] * w2 + bias
            bot = p_ref[bb, n - 8:n, :]
            nxt = jnp.where(edge == 7, 0.0, pltpu.roll(bot, 7, axis=0))
            dst_ref[bb, n - 8:n, :] = p_ref[bb, n - 9:n - 1, :] * w0 + bot * w1 + nxt * w2 + bias

    def samples(ref, start):
        rows = pl.ds(start, q, stride=2)
        return jnp.concatenate([ref.at[bb][rows, :] for bb in range(nbat)], axis=1)

    def forward(src_ref):
        for j in range(nb):
            e = _dot(fe_ref[...], samples(src_ref, j * p).astype(bf16))
            o = _dot(fo_ref[...], samples(src_ref, j * p + 1).astype(bf16))
            za_ref[j, 0:q] = e[:q] + o[:q]
            za_ref[j, q:p] = e[:q] - o[:q]
            zb_ref[j, 0:q] = e[q:] + o[q:]
            zb_ref[j, q:p] = o[q:] - e[q:]

    def filter_and_gate(src_ref, gate_ref, ha_ref, hb_ref, skip, dst_ref):
        for i in range(nb):
            for c in range(q // HY_ROWS):
                lo = slice(c * HY_ROWS, (c + 1) * HY_ROWS)
                hi = slice(q + c * HY_ROWS, q + (c + 1) * HY_ROWS)
                acc = []
                for rows in (lo, hi):
                    ya = [None] * nbat
                    yb = [None] * nbat
                    for j in range(nb):
                        d = i - j + (nb - 1)
                        h_a = ha_ref[d, rows, :]
                        h_b = hb_ref[d, rows, :]
                        for bb, lanes in enumerate(halves):
                            z_a = za_ref[j, rows, lanes]
                            z_b = zb_ref[j, rows, lanes]
                            t_a = h_a * z_a - h_b * z_b
                            t_b = h_a * z_b + h_b * z_a
                            ya[bb] = t_a if ya[bb] is None else ya[bb] + t_a
                            yb[bb] = t_b if yb[bb] is None else yb[bb] + t_b
                    acc.append((ya, yb))
                (ya_lo, yb_lo), (ya_hi, yb_hi) = acc
                for bb, lanes in enumerate(halves):
                    ye_ref[i, lo, lanes] = (ya_lo[bb] + ya_hi[bb]).astype(bf16)
                    ye_ref[i, hi, lanes] = (yb_lo[bb] - yb_hi[bb]).astype(bf16)
                    yo_ref[i, lo, lanes] = (ya_lo[bb] - ya_hi[bb]).astype(bf16)
                    yo_ref[i, hi, lanes] = (yb_lo[bb] + yb_hi[bb]).astype(bf16)
            for parity, y in ((0, _dot(ie_ref[...], ye_ref[i])), (1, _dot(io_ref[...], yo_ref[i]))):
                rows = pl.ds(i * p + parity, q, stride=2)
                for bb, lanes in enumerate(halves):
                    dst_ref.at[bb][rows, :] = gate_ref.at[bb][rows, :] * (
                        y[:, lanes] + src_ref.at[bb][rows, :] * skip)

    short_conv(pv_ref, 0, sv_ref)
    forward(sv_ref)
    short_conv(p1_ref, 1, s1_ref)
    short_conv(p2_ref, 2, s2_ref)
    filter_and_gate(sv_ref, s1_ref, ha0_ref, hb0_ref, skip_ref[0:1, :], u_ref)
    forward(u_ref)
    filter_and_gate(u_ref, s2_ref, ha1_ref, hb1_ref, skip_ref[1:2, :], sv_ref)
    for bb in range(nbat):
        o_ref[bb] = sv_ref[bb].astype(o_ref.dtype)


def _hyena(pb, conv_w3, conv_b3, skip, h_a, h_b, consts, p):
    b, n, _ = pb.shape
    tc = HY_TC
    nct = D // tc
    nb = n // p
    nd = 2 * nb - 1
    nbat = HY_NBAT[n]
    wide = nbat * tc

    def pspec(part):
        return pl.BlockSpec((nbat, n, tc), lambda c, b: (b, 0, part * nct + c))

    def hspec(order):
        return pl.BlockSpec((nd, p, tc), lambda c, b: (0, 0, order * nct + c),
                            pipeline_mode=pl.Buffered(1))

    return pl.pallas_call(
        functools.partial(_hyena_kernel, n=n, p=p),
        grid=(nct, b // nbat),
        in_specs=[
            pspec(0), pspec(1), pspec(2),
            pl.BlockSpec((3, 3, tc), lambda c, b: (0, 0, c)),
            pl.BlockSpec((3, 1, tc), lambda c, b: (0, 0, c)),
            pl.BlockSpec((2, tc), lambda c, b: (0, c)),
            hspec(0), hspec(0), hspec(1), hspec(1),
        ] + [_whole(a) for a in consts],
        out_specs=pl.BlockSpec((nbat, n, tc), lambda c, b: (b, 0, c)),
        out_shape=jax.ShapeDtypeStruct((b, n, D), bf16),
        scratch_shapes=(
            [pltpu.VMEM((nbat, n, tc), f32)] * 4
            + [pltpu.VMEM((nb, p, wide), f32)] * 2
            + [pltpu.VMEM((nb, p, wide), bf16)] * 2
        ),
        compiler_params=_params(("arbitrary", "arbitrary")),
        name="hyena",
    )(pb, pb, pb, conv_w3, conv_b3, skip, h_a, h_b, h_a, h_b, *consts)


def _mix_ffn_kernel(*refs, na, final):
    x_ref, gate1_ref = refs[:2]
    a_refs = refs[2:2 + na]
    wo_ref, g_ref, sh_ref, sc_ref, gate2_ref, wgu_ref, wd_ref = refs[2 + na:9 + na]
    rest = refs[9 + na:]
    fg_ref = rest[0] if final else None
    o_ref, x1_ref = rest[-2:]

    mix = _dot(a_refs[0][0], wo_ref[0:D, :])
    for k in range(1, na):
        mix = mix + _dot(a_refs[k][0], wo_ref[k * D:(k + 1) * D, :])
    x1_ref[...] = x_ref[0] + gate1_ref[0] * mix

    h = _norm_mod(x1_ref[...], g_ref[...], sc_ref[0], sh_ref[0]).astype(bf16)
    acc = None
    for f in range(D_FF // FFN_TF):
        cols = slice(f * FFN_TF, (f + 1) * FFN_TF)
        ucols = slice(D_FF + f * FFN_TF, D_FF + (f + 1) * FFN_TF)
        act = (jax.nn.silu(_dot(h, wgu_ref[:, cols])) * _dot(h, wgu_ref[:, ucols])).astype(bf16)
        part = _dot(act, wd_ref[cols, :])
        acc = part if acc is None else acc + part
    y = x1_ref[...] + gate2_ref[0] * acc
    if final:
        y = y * lax.rsqrt(jnp.mean(y * y, axis=-1, keepdims=True) + EPS) * fg_ref[...]
    o_ref[0] = y


def _mix_ffn(x, mods3, row_of, acts, w_o, g, w_gu, w_down, tm, final_g=None):
    b, n, _ = x.shape
    na = len(acts)
    assert w_o.shape == (na * D, D) and all(a.shape[-1] == D for a in acts)
    extra = [] if final_g is None else [final_g]
    return pl.pallas_call(
        functools.partial(_mix_ffn_kernel, na=na, final=final_g is not None),
        grid=(b, n // tm),
        in_specs=([_row_spec(tm, D), _mod_spec(row_of, 2)]
                  + [_row_spec(tm, D) for _ in acts]
                  + [_whole(w_o), _whole(g), _mod_spec(row_of, 3), _mod_spec(row_of, 4), _mod_spec(row_of, 5),
                     _whole(w_gu), _whole(w_down)] + [_whole(a) for a in extra]),
        out_specs=_row_spec(tm, D),
        out_shape=jax.ShapeDtypeStruct((b, n, D), f32),
        scratch_shapes=[pltpu.VMEM((tm, D), f32)],
        compiler_params=_params(("arbitrary", "arbitrary")),
        name="mix_ffn",
    )(x, mods3, *acts, w_o, g, mods3, mods3, mods3, w_gu, w_down, *extra)


def _rope_tables(n):
    rows = n // GRID_W
    row = np.repeat(np.arange(rows), GRID_W).astype(np.float64)
    col = np.tile(np.arange(GRID_W), rows).astype(np.float64)
    half = HEAD_DIM // 2
    inv = ROPE_THETA ** (-np.arange(0, half, 2, dtype=np.float64) / half)
    ang = np.concatenate([row[:, None] * inv, col[:, None] * inv], axis=-1)
    cos, sin = np.cos(ang), np.sin(ang)
    return (jnp.asarray(np.concatenate([cos, cos], axis=-1), f32),
            jnp.asarray(np.concatenate([-sin, sin], axis=-1), f32))


def _qkv_kernel(*refs, with_q, rope):
    x_ref, g_ref, sh_ref, sc_ref, w_ref, qg_ref, kg_ref = refs[:7]
    pos = 7
    if rope:
        cos_ref, sin_ref = refs[7:9]
        pos = 9
    out_refs = refs[pos:]
    h = _norm_mod(x_ref[0], g_ref[...], sc_ref[0], sh_ref[0]).astype(bf16)
    qkv = _dot(h, w_ref[...] if with_q else w_ref[:, N_HEADS * HEAD_DIM:])
    ones = jnp.ones((HEAD_DIM, HEAD_DIM), bf16)

    def head(t, gain, scale):
        sq_hi, sq_lo = _split_bf16(t * t)
        ssq = _dot(sq_hi, ones) + _dot(sq_lo, ones)
        t = t * lax.rsqrt(ssq * (1.0 / HEAD_DIM) + EPS) * gain
        if rope:
            t = t * cos_ref[...] + pltpu.roll(t, HEAD_DIM // 2, axis=1) * sin_ref[...]
        return (t * scale).astype(bf16)

    off = 0
    if with_q:
        q_ref, k_ref, v_ref = out_refs
        for hd in range(N_HEADS):
            cols = slice(hd * HEAD_DIM, (hd + 1) * HEAD_DIM)
            q_ref[0, :, cols] = head(qkv[:, cols], qg_ref[...], HEAD_DIM ** -0.5)
        off = N_HEADS * HEAD_DIM
    else:
        k_ref, v_ref = out_refs
    for kv in range(N_KV):
        cols = slice(kv * HEAD_DIM, (kv + 1) * HEAD_DIM)
        src = slice(off + kv * HEAD_DIM, off + (kv + 1) * HEAD_DIM)
        k_ref[0, :, cols] = head(qkv[:, src], kg_ref[...], 1.0)
    voff = off + N_KV * HEAD_DIM
    for kv in range(N_KV):
        src = slice(voff + kv * HEAD_DIM, voff + (kv + 1) * HEAD_DIM)
        v_ref[0, :, 2 * kv * HEAD_DIM:(2 * kv + 1) * HEAD_DIM] = qkv[:, src].astype(bf16)
        v_ref[0, :, (2 * kv + 1) * HEAD_DIM:(2 * kv + 2) * HEAD_DIM] = jnp.ones(
            (qkv.shape[0], HEAD_DIM), bf16)


def _qkv(x, g, mods3, row_of, w, q_g, k_g, tables, with_q, tm):
    b, n, _ = x.shape
    nq = N_HEADS * HEAD_DIM
    nk = N_KV * HEAD_DIM
    rope = tables is not None
    in_specs = [_row_spec(tm, D), _whole(g), _mod_spec(row_of, 0), _mod_spec(row_of, 1),
                _whole(w), _whole(q_g), _whole(k_g)]
    args = [x, g, mods3, mods3, w, q_g, k_g]
    if rope:
        in_specs += [pl.BlockSpec((tm, HEAD_DIM), lambda b, i: (i, 0))] * 2
        args += list(tables)
    out_specs = [_row_spec(tm, nk), _row_spec(tm, 2 * nk)]
    out_shape = [jax.ShapeDtypeStruct((b, n, nk), bf16), jax.ShapeDtypeStruct((b, n, 2 * nk), bf16)]
    if with_q:
        out_specs = [_row_spec(tm, nq)] + out_specs
        out_shape = [jax.ShapeDtypeStruct((b, n, nq), bf16)] + out_shape
    return pl.pallas_call(
        functools.partial(_qkv_kernel, with_q=with_q, rope=rope),
        grid=(b, n // tm),
        in_specs=in_specs,
        out_specs=out_specs,
        out_shape=out_shape,
        compiler_params=_params(("arbitrary", "arbitrary")),
        name="qkv",
    )(*args)


def _attn_kernel(*refs, nseg):
    q_ref = refs[0]
    k_refs = refs[1:1 + nseg]
    v_refs = refs[1 + nseg:1 + 2 * nseg]
    o_ref = refs[1 + 2 * nseg]
    def scores(hd):
        q = q_ref[0, :, hd * HEAD_DIM:(hd + 1) * HEAD_DIM]
        return [lax.dot_general(q, k[0], _NT, preferred_element_type=f32) for k in k_refs]

    s_next = scores(0)
    for hd in range(GQA):
        s = s_next
        if hd + 1 < GQA:
            s_next = scores(hd + 1)
        m = jnp.max(s[0], axis=-1, keepdims=True)
        for t in s[1:]:
            m = jnp.maximum(m, jnp.max(t, axis=-1, keepdims=True))
        ov = _dot(jnp.exp(s[0] - m).astype(bf16), v_refs[0][0])
        for t, v in zip(s[1:], v_refs[1:]):
            ov = ov + _dot(jnp.exp(t - m).astype(bf16), v[0])
        o_ref[0, :, hd * HEAD_DIM:(hd + 1) * HEAD_DIM] = (
            ov[:, :HEAD_DIM] / ov[:, HEAD_DIM:]).astype(o_ref.dtype)


def _attention(q, ks, vs, tq):
    b, n, _ = q.shape
    nseg = len(ks)
    gw = GQA * HEAD_DIM
    k_specs = [pl.BlockSpec((1, k.shape[1], HEAD_DIM), lambda b, h, i: (b, 0, h)) for k in ks]
    v_specs = [pl.BlockSpec((1, v.shape[1], 2 * HEAD_DIM), lambda b, h, i: (b, 0, h)) for v in vs]
    return pl.pallas_call(
        functools.partial(_attn_kernel, nseg=nseg),
        grid=(b, N_KV, n // tq),
        in_specs=[pl.BlockSpec((1, tq, gw), lambda b, h, i: (b, i, h))] + k_specs + v_specs,
        out_specs=pl.BlockSpec((1, tq, gw), lambda b, h, i: (b, i, h)),
        out_shape=jax.ShapeDtypeStruct((b, n, N_HEADS * HEAD_DIM), bf16),
        compiler_params=_params(("arbitrary", "arbitrary", "arbitrary")),
        name="attn",
    )(q, *ks, *vs)


def _pad2(a, rows, cols):
    return jnp.pad(a, ((0, rows - a.shape[0]), (0, cols - a.shape[1])))


def kernel(x, c, ctx, c_ctx, mod_w, mod_b, norm1_g, norm2_g, ffn_w_gu, ffn_w_down, even_w_in,
           gmlp_ln_g, gmlp_ln_b, gmlp_w_s, gmlp_b_s, hyena_conv_w, hyena_conv_b, hyena_f_w1,
           hyena_f_b1, hyena_f_w2, hyena_f_b2, hyena_f_w3, hyena_freq, hyena_skip, even_w_out,
           attn_w_qkv, attn_q_g, attn_k_g, attn_w_o, final_g):
    lat_row = lambda b: b
    ctx_row = lambda b: CTX_ROW
    streams = {"lat": (SEQ, lat_row, 512), "ctx": (CTX, ctx_row, 512)}

    cond = jnp.zeros((MOD_ROWS, D), f32).at[:NB].set(c).at[CTX_ROW].set(c_ctx)
    mods = _mods(cond, mod_w, mod_b)

    dft = {p: _dft_consts(p) for p in set(HY_P.values())}
    fconst = {n: _filter_consts(n) for n in (SEQ, CTX)}
    rope = _rope_tables(SEQ)
    nq = N_HEADS * HEAD_DIM
    nk = N_KV * HEAD_DIM

    xs = {"lat": x, "ctx": ctx.reshape(1, NB * CTX, D)}
    for layer in range(DEPTH):
        last = layer == DEPTH - 1
        is_even = layer % 2 == 0
        mods3 = mods[layer].reshape(MOD_ROWS, 1, 6 * D)
        g1 = norm1_g[layer].reshape(1, D)
        g2 = norm2_g[layer].reshape(1, D)
        w_gu = ffn_w_gu[layer].astype(bf16)
        w_down = ffn_w_down[layer].astype(bf16)
        mixed = {}
        if is_even:
            i = layer // 2
            w_in = even_w_in[i].astype(bf16)
            w_out = even_w_out[i].astype(bf16)
            w_s = gmlp_w_s[i].astype(bf16)
            b_s_rows = jnp.repeat(gmlp_b_s[i].T, CHUNK, axis=1)
            ln_g = gmlp_ln_g[i].reshape(1, D)
            ln_b = gmlp_ln_b[i].reshape(1, D)
            conv_w3 = hyena_conv_w[i].reshape(3, 3, D).transpose(1, 0, 2)
            conv_b3 = hyena_conv_b[i].reshape(3, 1, D)
            w1p = _pad2(hyena_f_w1[i], LANES, LANES)
            w2p = _pad2(hyena_f_w2[i], LANES, LANES)
            b1p = _pad2(hyena_f_b1[i][None, :], 1, LANES)
            b2p = _pad2(hyena_f_b2[i][None, :], 1, LANES)
            f0p = _pad2(hyena_freq[i, 0][None, :], 1, LANES)
            f1p = _pad2(hyena_freq[i, 1][None, :], 1, LANES)
            w3p = jnp.pad(hyena_f_w3[i].reshape(FILTER_HIDDEN, 2, 2 * D).transpose(1, 0, 2),
                          ((0, 0), (0, LANES - FILTER_HIDDEN), (0, 0)))
            for key, (n, row_of, tm) in streams.items():
                if key == "ctx" and last:
                    continue
                xc = xs[key]
                p = HY_P[n]
                *hy_consts, gmat = dft[p]
                zemb, deltas = fconst[n]
                hraw, hsum = _filter_time(n, zemb, deltas, w1p, b1p, w2p, b2p, f0p, f1p, w3p)
                h_a, h_b = _filter_spec(n, p, gmat, hraw, hsum)
                y_a, pb = _even_in(xc, g1, mods3, row_of, w_in, ln_g, ln_b, w_s, b_s_rows, tm)
                z_b = _hyena(pb.reshape(NB, n, 3 * D), conv_w3, conv_b3, hyena_skip[i],
                             h_a, h_b, hy_consts, p)
                mixed[key] = ([y_a, z_b.reshape(y_a.shape)], w_out)
        else:
            j = layer // 2
            w_qkv = attn_w_qkv[j].astype(bf16)
            w_o = attn_w_o[j].astype(bf16)
            q_g = attn_q_g[j].reshape(1, HEAD_DIM)
            k_g = attn_k_g[j].reshape(1, HEAD_DIM)
            q_l, k_l, v_l = _qkv(xs["lat"], g1, mods3, lat_row, w_qkv, q_g, k_g, rope, True, 1024)
            if last:
                k_c, v_c = _qkv(xs["ctx"], g1, mods3, ctx_row, w_qkv, q_g, k_g, None, False, 512)
            else:
                q_c, k_c, v_c = _qkv(xs["ctx"], g1, mods3, ctx_row, w_qkv, q_g, k_g, None, True, 512)
            k_c = k_c.reshape(NB, CTX, nk)
            v_c = v_c.reshape(NB, CTX, 2 * nk)
            mixed["lat"] = ([_attention(q_l, [k_c, k_l], [v_c, v_l], 1024)], w_o)
            if not last:
                o_c = _attention(q_c.reshape(NB, CTX, nq), [k_c], [v_c], CTX)
                mixed["ctx"] = ([o_c.reshape(1, NB * CTX, nq)], w_o)
        for key, (acts, w_mix) in mixed.items():
            _, row_of, tm = streams[key]
            fg = final_g.reshape(1, D) if last else None
            xs[key] = _mix_ffn(xs[key], mods3, row_of, acts, w_mix, g2, w_gu, w_down, tm, fg)
    return xs["lat"]
```

```python
import functools
import math

import numpy as np
import jax
import jax.numpy as jnp
from jax import lax
from jax.experimental import pallas as pl
from jax.experimental.pallas import tpu as pltpu

D = 1024
NB = 16
SEQ = 2048
CTX = 256
DEPTH = 4
GRID_W = 64
EPS = 1e-6
CHUNK = 128
A_GROUPS = 8
HEAD_DIM = 128
N_HEADS = 8
N_KV = 2
GQA = N_HEADS // N_KV
D_FF = 2816
FILTER_EMB = 33
FILTER_BANDS = 16
FILTER_HIDDEN = 64
MIN_DECAY = math.log(1e-2) / 1.5
MAX_DECAY = math.log(1e-2) / 0.3
ROPE_THETA = 10000.0

LANES = 128
CTX_ROW = NB
MOD_ROWS = 24
HY_P = {SEQ: 1024, CTX: 256}
HY_TC = LANES
HY_NBAT = {SEQ: 2, CTX: 8}
HY_ROWS = 32
FFN_TF = 256
VMEM_LIMIT = 56 * 1024 * 1024

bf16 = jnp.bfloat16
f32 = jnp.float32


_NT = (((1,), (1,)), ((), ()))


def _dot(a, b):
    return jnp.dot(a, b, preferred_element_type=f32)


def _mm_f32(a, b):
    return jnp.dot(a, b, preferred_element_type=f32, precision=lax.Precision.HIGHEST)


def _params(sem):
    return pltpu.CompilerParams(dimension_semantics=sem, vmem_limit_bytes=VMEM_LIMIT)


def _whole(a):
    nd = a.ndim
    return pl.BlockSpec(a.shape, lambda *_: (0,) * nd)


def _layer_spec(w):
    arr, idx = w
    nd = arr.ndim - 1
    return pl.BlockSpec((None,) + arr.shape[1:], lambda *_: (idx,) + (0,) * nd,
                        pipeline_mode=pl.Buffered(1))


def _mods_kernel(cond_ref, w_ref, b_ref, o_ref):
    a = jax.nn.silu(cond_ref[...]).astype(bf16)
    o_ref[0] = _dot(a, w_ref[0].astype(bf16)) + b_ref[0]


def _mods(cond, mod_w, mod_b):
    tn = 1024
    n6 = 6 * D
    return pl.pallas_call(
        _mods_kernel,
        grid=(DEPTH, n6 // tn),
        in_specs=[
            pl.BlockSpec((MOD_ROWS, D), lambda l, j: (0, 0)),
            pl.BlockSpec((1, D, tn), lambda l, j: (l, 0, j)),
            pl.BlockSpec((1, 1, tn), lambda l, j: (l, 0, j)),
        ],
        out_specs=pl.BlockSpec((1, MOD_ROWS, tn), lambda l, j: (l, 0, j)),
        out_shape=jax.ShapeDtypeStruct((DEPTH, MOD_ROWS, n6), f32),
        compiler_params=_params(("arbitrary", "arbitrary")),
        name="mods",
    )(cond, mod_w, mod_b.reshape(DEPTH, 1, n6))


def _mod_spec(row_of, chunk):
    return pl.BlockSpec((1, 1, D), lambda b, i: (row_of(b), 0, chunk))


def _row_spec(tm, width):
    return pl.BlockSpec((1, tm, width), lambda b, i: (b, i, 0))


def _norm_mod(x, g, sc, sh):
    y = x * lax.rsqrt(jnp.mean(x * x, axis=-1, keepdims=True) + EPS)
    return (y * g) * (1.0 + sc) + sh


def _gelu(x):
    return 0.5 * x * (1.0 + lax.erf(x * (2.0 ** -0.5)))


def _even_in_kernel(x_ref, g_ref, sh_ref, sc_ref, w_ref, lng_ref, lnb_ref, ws_ref, bs_ref,
                    ya_ref, pb_ref, *, tm):
    h = _norm_mod(x_ref[0], g_ref[...], sc_ref[0], sh_ref[0]).astype(bf16)
    v = _gelu(_dot(h, w_ref[:, D:2 * D]))
    u = _gelu(_dot(h, w_ref[:, :D]))
    pb_ref[0] = _dot(h, w_ref[:, 2 * D:])
    mu = jnp.mean(v, axis=-1, keepdims=True)
    vc = v - mu
    var = jnp.mean(vc * vc, axis=-1, keepdims=True)
    vn = (vc * lax.rsqrt(var + 1e-5) * lng_ref[...] + lnb_ref[...]).astype(bf16)
    nck = tm // CHUNK
    for g in range(A_GROUPS):
        cols = slice(g * CHUNK, (g + 1) * CHUNK)
        rhs = jnp.concatenate([vn[k * CHUNK:(k + 1) * CHUNK, cols] for k in range(nck)], axis=1)
        sv = _dot(ws_ref[g], rhs)
        for k in range(nck):
            rows = slice(k * CHUNK, (k + 1) * CHUNK)
            gate = sv[:, k * CHUNK:(k + 1) * CHUNK] + bs_ref[:, cols]
            ya_ref[0, rows, cols] = (u[rows, cols] * gate).astype(bf16)


def _even_in(x, g, mods3, row_of, w_in, ln_g, ln_b, w_s, b_s_rows, tm):
    b, n, _ = x.shape
    return pl.pallas_call(
        functools.partial(_even_in_kernel, tm=tm),
        grid=(b, n // tm),
        in_specs=[
            _row_spec(tm, D), _whole(g), _mod_spec(row_of, 0), _mod_spec(row_of, 1),
            _layer_spec(w_in), _whole(ln_g), _whole(ln_b), _layer_spec(w_s), _whole(b_s_rows),
        ],
        out_specs=[_row_spec(tm, D), _row_spec(tm, 3 * D)],
        out_shape=[jax.ShapeDtypeStruct((b, n, D), bf16), jax.ShapeDtypeStruct((b, n, 3 * D), f32)],
        compiler_params=_params(("arbitrary", "arbitrary")),
        name="even_in",
    )(x, g, mods3, mods3, w_in[0], ln_g, ln_b, w_s[0], b_s_rows)


def _filter_consts(n):
    lag = np.abs(np.arange(2 * n) - n)
    lag[0] = 0
    t = np.linspace(0.0, 1.0, n)[lag][:, None]
    w = (2.0 * math.pi / n) * lag[:, None]
    bands = np.linspace(1e-4, FILTER_BANDS - 1, FILTER_BANDS)[None, :]
    ang = bands * w
    z = np.concatenate([t, np.cos(ang), -np.sin(ang)], axis=-1)
    zp = np.zeros((2 * n, LANES), np.float64)
    zp[:, :FILTER_EMB] = z
    deltas = np.abs(np.linspace(MIN_DECAY, MAX_DECAY, D))
    return jnp.asarray(zp, f32), jnp.asarray(np.tile(deltas, 2)[None, :], f32)


def _filter_time_kernel(z_ref, w1_ref, b1_ref, w2_ref, b2_ref, f0_ref, f1_ref, w3_ref, dl_ref,
                        h_ref, s_ref, *, tr):
    r = pl.program_id(0)
    z = z_ref[...]
    hid = jnp.sin(f0_ref[...] * (_mm_f32(z, w1_ref[...]) + b1_ref[...]))
    hid = jnp.sin(f1_ref[...] * (_mm_f32(hid, w2_ref[...]) + b2_ref[...]))
    h = _mm_f32(hid, w3_ref[0])
    h = h * jnp.exp(-z[:, 0:1] * dl_ref[...])
    row = lax.broadcasted_iota(jnp.int32, h.shape, 0) + r * tr
    h = jnp.where(row == 0, 0.0, h)
    h_ref[...] = h

    @pl.when(r == 0)
    def _():
        s_ref[...] = jnp.zeros_like(s_ref)

    s_ref[...] += jnp.sum(jnp.abs(h), axis=0, keepdims=True)


def _filter_time(n, zemb, deltas, w1p, b1p, w2p, b2p, f0p, f1p, w3p):
    tr = 256
    half = n // tr
    c2 = 2 * D
    return pl.pallas_call(
        functools.partial(_filter_time_kernel, tr=tr),
        grid=(2 * n // tr,),
        in_specs=[
            pl.BlockSpec((tr, LANES), lambda r: (r, 0)),
            _whole(w1p), _whole(b1p), _whole(w2p), _whole(b2p), _whole(f0p), _whole(f1p),
            pl.BlockSpec((1, LANES, c2), lambda r: (jnp.where(r < half, 1, 0), 0, 0)),
            _whole(deltas),
        ],
        out_specs=[pl.BlockSpec((tr, c2), lambda r: (r, 0)), pl.BlockSpec((1, c2), lambda r: (0, 0))],
        out_shape=[jax.ShapeDtypeStruct((2 * n, c2), f32), jax.ShapeDtypeStruct((1, c2), f32)],
        compiler_params=_params(("arbitrary",)),
        name="filter_time",
    )(zemb, w1p, b1p, w2p, b2p, f0p, f1p, w3p, deltas)


def _split_bf16(a):
    hi = a.astype(bf16)
    return hi, (a - hi.astype(f32)).astype(bf16)


def _dft_consts(p):
    q = p // 2
    theta = np.pi * (2 * np.arange(q) + 1) / (2 * p)
    te = theta[:, None] * (2 * np.arange(q))[None, :]
    to = theta[:, None] * (2 * np.arange(q) + 1)[None, :]
    fe = np.concatenate([np.cos(te), -np.sin(te)], axis=0)
    fo = np.concatenate([np.cos(to), -np.sin(to)], axis=0)
    ie = np.concatenate([np.cos(te).T, -np.sin(te).T], axis=1) / p
    io = np.concatenate([np.cos(to).T, -np.sin(to).T], axis=1) / p
    freq = np.concatenate([np.arange(q), p - 1 - np.arange(q)])
    tf = (np.pi * (2 * freq + 1) / (2 * p))[:, None] * np.arange(-p, p)[None, :]
    g = np.concatenate([np.cos(tf), -np.sin(tf)], axis=0)
    g[:, 0] = 0.0
    return tuple(jnp.asarray(a, bf16) for a in (fe, fo, ie, io, g))


def _filter_spec_kernel(g_ref, lo_ref, hi_ref, s_ref, ha_ref, hb_ref, *, p):
    h = (_dot(g_ref[:, :p], lo_ref[...].astype(bf16))
         + _dot(g_ref[:, p:], hi_ref[...].astype(bf16)))
    h = h / s_ref[...]
    ha_ref[0] = h[:p]
    hb_ref[0] = h[p:]


def _filter_spec(n, p, gmat, hraw, hsum):
    nb = n // p
    c2 = 2 * D
    tc = 512
    nd = 2 * nb - 1
    out = jax.ShapeDtypeStruct((nd, p, c2), f32)
    return pl.pallas_call(
        functools.partial(_filter_spec_kernel, p=p),
        grid=(nd, c2 // tc),
        in_specs=[
            _whole(gmat),
            pl.BlockSpec((p, tc), lambda d, c: (d, c)),
            pl.BlockSpec((p, tc), lambda d, c: (d + 1, c)),
            pl.BlockSpec((1, tc), lambda d, c: (0, c)),
        ],
        out_specs=[pl.BlockSpec((1, p, tc), lambda d, c: (d, 0, c))] * 2,
        out_shape=[out, out],
        compiler_params=_params(("arbitrary", "arbitrary")),
        name="filter_spec",
    )(gmat, hraw, hraw, hsum)


def _hyena_kernel(pv_ref, p1_ref, p2_ref, cw_ref, cb_ref, skip_ref,
                  ha0_ref, hb0_ref, ha1_ref, hb1_ref, fe_ref, fo_ref, ie_ref, io_ref,
                  o_ref, sv_ref, s1_ref, s2_ref, u_ref, za_ref, zb_ref, ye_ref, yo_ref, *, n, p):
    nb = n // p
    q = p // 2
    tc = HY_TC
    nbat = o_ref.shape[0]
    halves = [slice(bb * tc, (bb + 1) * tc) for bb in range(nbat)]

    def short_conv(p_ref, part, dst_ref):
        w = cw_ref[part]
        w0, w1, w2, bias = w[0:1], w[1:2], w[2:3], cb_ref[part]
        edge = lax.broadcasted_iota(jnp.int32, (8, tc), 0)
        for bb in range(nbat):
            dst_ref[bb, 8:n - 8, :] = (p_ref[bb, 7:n - 9, :] * w0 + p_ref[bb, 8:n - 8, :] * w1
                                       + p_ref[bb, 9:n - 7, :] * w2 + bias)
            top = p_ref[bb, 0:8, :]
            prev = jnp.where(edge == 0, 0.0, pltpu.roll(top, 1, axis=0))
            dst_ref[bb, 0:8, :] = prev * w0 + top * w1 + p_ref[bb, 1:9, :] * w2 + bias
            bot = p_ref[bb, n - 8:n, :]
            nxt = jnp.where(edge == 7, 0.0, pltpu.roll(bot, 7, axis=0))
            dst_ref[bb, n - 8:n, :] = p_ref[bb, n - 9:n - 1, :] * w0 + bot * w1 + nxt * w2 + bias

    def samples(ref, start):
        rows = pl.ds(start, q, stride=2)
        return jnp.concatenate([ref.at[bb][rows, :] for bb in range(nbat)], axis=1)

    def forward(src_ref):
        for j in range(nb):
            e = _dot(fe_ref[...], samples(src_ref, j * p).astype(bf16))
            o = _dot(fo_ref[...], samples(src_ref, j * p + 1).astype(bf16))
            za_ref[j, 0:q] = e[:q] + o[:q]
            za_ref[j, q:p] = e[:q] - o[:q]
            zb_ref[j, 0:q] = e[q:] + o[q:]
            zb_ref[j, q:p] = o[q:] - e[q:]

    def filter_and_gate(src_ref, gate_ref, ha_ref, hb_ref, skip, dst_ref):
        for i in range(nb):
            for c in range(q // HY_ROWS):
                lo = slice(c * HY_ROWS, (c + 1) * HY_ROWS)
                hi = slice(q + c * HY_ROWS, q + (c + 1) * HY_ROWS)
                acc = []
                for rows in (lo, hi):
                    ya = [None] * nbat
                    yb = [None] * nbat
                    for j in range(nb):
                        d = i - j + (nb - 1)
                        h_a = ha_ref[d, rows, :]
                        h_b = hb_ref[d, rows, :]
                        for bb, lanes in enumerate(halves):
                            z_a = za_ref[j, rows, lanes]
                            z_b = zb_ref[j, rows, lanes]
                            t_a = h_a * z_a - h_b * z_b
                            t_b = h_a * z_b + h_b * z_a
                            ya[bb] = t_a if ya[bb] is None else ya[bb] + t_a
                            yb[bb] = t_b if yb[bb] is None else yb[bb] + t_b
                    acc.append((ya, yb))
                (ya_lo, yb_lo), (ya_hi, yb_hi) = acc
                for bb, lanes in enumerate(halves):
                    ye_ref[i, lo, lanes] = (ya_lo[bb] + ya_hi[bb]).astype(bf16)
                    ye_ref[i, hi, lanes] = (yb_lo[bb] - yb_hi[bb]).astype(bf16)
                    yo_ref[i, lo, lanes] = (ya_lo[bb] - ya_hi[bb]).astype(bf16)
                    yo_ref[i, hi, lanes] = (yb_lo[bb] + yb_hi[bb]).astype(bf16)
            for parity, y in ((0, _dot(ie_ref[...], ye_ref[i])), (1, _dot(io_ref[...], yo_ref[i]))):
                rows = pl.ds(i * p + parity, q, stride=2)
                for bb, lanes in enumerate(halves):
                    dst_ref.at[bb][rows, :] = gate_ref.at[bb][rows, :] * (
                        y[:, lanes] + src_ref.at[bb][rows, :] * skip)

    short_conv(pv_ref, 0, sv_ref)
    forward(sv_ref)
    short_conv(p1_ref, 1, s1_ref)
    short_conv(p2_ref, 2, s2_ref)
    filter_and_gate(sv_ref, s1_ref, ha0_ref, hb0_ref, skip_ref[0:1, :], u_ref)
    forward(u_ref)
    filter_and_gate(u_ref, s2_ref, ha1_ref, hb1_ref, skip_ref[1:2, :], sv_ref)
    for bb in range(nbat):
        o_ref[bb] = sv_ref[bb].astype(o_ref.dtype)


def _hyena(pb, conv_w3, conv_b3, skip, h_a, h_b, consts, p):
    b, n, _ = pb.shape
    tc = HY_TC
    nct = D // tc
    nb = n // p
    nd = 2 * nb - 1
    nbat = HY_NBAT[n]
    wide = nbat * tc

    def pspec(part):
        return pl.BlockSpec((nbat, n, tc), lambda c, b: (b, 0, part * nct + c))

    def hspec(order):
        return pl.BlockSpec((nd, p, tc), lambda c, b: (0, 0, order * nct + c))

    return pl.pallas_call(
        functools.partial(_hyena_kernel, n=n, p=p),
        grid=(nct, b // nbat),
        in_specs=[
            pspec(0), pspec(1), pspec(2),
            pl.BlockSpec((3, 3, tc), lambda c, b: (0, 0, c)),
            pl.BlockSpec((3, 1, tc), lambda c, b: (0, 0, c)),
            pl.BlockSpec((2, tc), lambda c, b: (0, c)),
            hspec(0), hspec(0), hspec(1), hspec(1),
        ] + [_whole(a) for a in consts],
        out_specs=pl.BlockSpec((nbat, n, tc), lambda c, b: (b, 0, c)),
        out_shape=jax.ShapeDtypeStruct((b, n, D), bf16),
        scratch_shapes=(
            [pltpu.VMEM((nbat, n, tc), f32)] * 4
            + [pltpu.VMEM((nb, p, wide), f32)] * 2
            + [pltpu.VMEM((nb, p, wide), bf16)] * 2
        ),
        compiler_params=_params(("arbitrary", "arbitrary")),
        name="hyena",
    )(pb, pb, pb, conv_w3, conv_b3, skip, h_a, h_b, h_a, h_b, *consts)


def _mix_ffn_kernel(*refs, na, final):
    x_ref, gate1_ref = refs[:2]
    a_refs = refs[2:2 + na]
    wo_ref, g_ref, sh_ref, sc_ref, gate2_ref, wgu_ref, wd_ref = refs[2 + na:9 + na]
    rest = refs[9 + na:]
    fg_ref = rest[0] if final else None
    o_ref, x1_ref = rest[-2:]

    mix = _dot(a_refs[0][0], wo_ref[0:D, :])
    for k in range(1, na):
        mix = mix + _dot(a_refs[k][0], wo_ref[k * D:(k + 1) * D, :])
    x1_ref[...] = x_ref[0] + gate1_ref[0] * mix

    h = _norm_mod(x1_ref[...], g_ref[...], sc_ref[0], sh_ref[0]).astype(bf16)
    acc = None
    for f in range(D_FF // FFN_TF):
        cols = slice(f * FFN_TF, (f + 1) * FFN_TF)
        ucols = slice(D_FF + f * FFN_TF, D_FF + (f + 1) * FFN_TF)
        act = (jax.nn.silu(_dot(h, wgu_ref[:, cols])) * _dot(h, wgu_ref[:, ucols])).astype(bf16)
        part = _dot(act, wd_ref[cols, :])
        acc = part if acc is None else acc + part
    y = x1_ref[...] + gate2_ref[0] * acc
    if final:
        y = y * lax.rsqrt(jnp.mean(y * y, axis=-1, keepdims=True) + EPS) * fg_ref[...]
    o_ref[0] = y


def _mix_ffn(x, mods3, row_of, acts, w_o, g, w_gu, w_down, tm, final_g=None):
    b, n, _ = x.shape
    na = len(acts)
    assert w_o[0].shape[1:] == (na * D, D) and all(a.shape[-1] == D for a in acts)
    extra = [] if final_g is None else [final_g]
    return pl.pallas_call(
        functools.partial(_mix_ffn_kernel, na=na, final=final_g is not None),
        grid=(b, n // tm),
        in_specs=([_row_spec(tm, D), _mod_spec(row_of, 2)]
                  + [_row_spec(tm, D) for _ in acts]
                  + [_layer_spec(w_o), _whole(g), _mod_spec(row_of, 3), _mod_spec(row_of, 4),
                     _mod_spec(row_of, 5), _layer_spec(w_gu), _layer_spec(w_down)]
                  + [_whole(a) for a in extra]),
        out_specs=_row_spec(tm, D),
        out_shape=jax.ShapeDtypeStruct((b, n, D), f32),
        scratch_shapes=[pltpu.VMEM((tm, D), f32)],
        compiler_params=_params(("arbitrary", "arbitrary")),
        name="mix_ffn",
    )(x, mods3, *acts, w_o[0], g, mods3, mods3, mods3, w_gu[0], w_down[0], *extra)


def _rope_tables(n):
    rows = n // GRID_W
    row = np.repeat(np.arange(rows), GRID_W).astype(np.float64)
    col = np.tile(np.arange(GRID_W), rows).astype(np.float64)
    half = HEAD_DIM // 2
    inv = ROPE_THETA ** (-np.arange(0, half, 2, dtype=np.float64) / half)
    ang = np.concatenate([row[:, None] * inv, col[:, None] * inv], axis=-1)
    cos, sin = np.cos(ang), np.sin(ang)
    return (jnp.asarray(np.concatenate([cos, cos], axis=-1), f32),
            jnp.asarray(np.concatenate([-sin, sin], axis=-1), f32))


def _qkv_kernel(*refs, with_q, rope):
    x_ref, g_ref, sh_ref, sc_ref, w_ref, qg_ref, kg_ref = refs[:7]
    pos = 7
    if rope:
        cos_ref, sin_ref = refs[7:9]
        pos = 9
    out_refs = refs[pos:]
    h = _norm_mod(x_ref[0], g_ref[...], sc_ref[0], sh_ref[0]).astype(bf16)
    qkv = _dot(h, w_ref[...] if with_q else w_ref[:, N_HEADS * HEAD_DIM:])
    ones = jnp.ones((HEAD_DIM, HEAD_DIM), bf16)

    def head(t, gain, scale):
        sq_hi, sq_lo = _split_bf16(t * t)
        ssq = _dot(sq_hi, ones) + _dot(sq_lo, ones)
        t = t * lax.rsqrt(ssq * (1.0 / HEAD_DIM) + EPS) * gain
        if rope:
            t = t * cos_ref[...] + pltpu.roll(t, HEAD_DIM // 2, axis=1) * sin_ref[...]
        return (t * scale).astype(bf16)

    off = 0
    if with_q:
        q_ref, k_ref, v_ref = out_refs
        for hd in range(N_HEADS):
            cols = slice(hd * HEAD_DIM, (hd + 1) * HEAD_DIM)
            q_ref[0, :, cols] = head(qkv[:, cols], qg_ref[...], HEAD_DIM ** -0.5)
        off = N_HEADS * HEAD_DIM
    else:
        k_ref, v_ref = out_refs
    for kv in range(N_KV):
        cols = slice(kv * HEAD_DIM, (kv + 1) * HEAD_DIM)
        src = slice(off + kv * HEAD_DIM, off + (kv + 1) * HEAD_DIM)
        k_ref[0, :, cols] = head(qkv[:, src], kg_ref[...], 1.0)
    voff = off + N_KV * HEAD_DIM
    for kv in range(N_KV):
        src = slice(voff + kv * HEAD_DIM, voff + (kv + 1) * HEAD_DIM)
        v_ref[0, :, 2 * kv * HEAD_DIM:(2 * kv + 1) * HEAD_DIM] = qkv[:, src].astype(bf16)
        v_ref[0, :, (2 * kv + 1) * HEAD_DIM:(2 * kv + 2) * HEAD_DIM] = jnp.ones(
            (qkv.shape[0], HEAD_DIM), bf16)


def _qkv(x, g, mods3, row_of, w, q_g, k_g, tables, with_q, tm):
    b, n, _ = x.shape
    nq = N_HEADS * HEAD_DIM
    nk = N_KV * HEAD_DIM
    rope = tables is not None
    in_specs = [_row_spec(tm, D), _whole(g), _mod_spec(row_of, 0), _mod_spec(row_of, 1),
                _layer_spec(w), _whole(q_g), _whole(k_g)]
    args = [x, g, mods3, mods3, w[0], q_g, k_g]
    if rope:
        in_specs += [pl.BlockSpec((tm, HEAD_DIM), lambda b, i: (i, 0))] * 2
        args += list(tables)
    out_specs = [_row_spec(tm, nk), _row_spec(tm, 2 * nk)]
    out_shape = [jax.ShapeDtypeStruct((b, n, nk), bf16), jax.ShapeDtypeStruct((b, n, 2 * nk), bf16)]
    if with_q:
        out_specs = [_row_spec(tm, nq)] + out_specs
        out_shape = [jax.ShapeDtypeStruct((b, n, nq), bf16)] + out_shape
    return pl.pallas_call(
        functools.partial(_qkv_kernel, with_q=with_q, rope=rope),
        grid=(b, n // tm),
        in_specs=in_specs,
        out_specs=out_specs,
        out_shape=out_shape,
        compiler_params=_params(("arbitrary", "arbitrary")),
        name="qkv",
    )(*args)


def _attn_kernel(*refs, nseg):
    q_ref = refs[0]
    k_refs = refs[1:1 + nseg]
    v_refs = refs[1 + nseg:1 + 2 * nseg]
    o_ref = refs[1 + 2 * nseg]
    def scores(hd):
        q = q_ref[0, :, hd * HEAD_DIM:(hd + 1) * HEAD_DIM]
        return [lax.dot_general(q, k[0], _NT, preferred_element_type=f32) for k in k_refs]

    s_next = scores(0)
    for hd in range(GQA):
        s = s_next
        if hd + 1 < GQA:
            s_next = scores(hd + 1)
        m = jnp.max(s[0], axis=-1, keepdims=True)
        for t in s[1:]:
            m = jnp.maximum(m, jnp.max(t, axis=-1, keepdims=True))
        ov = _dot(jnp.exp(s[0] - m).astype(bf16), v_refs[0][0])
        for t, v in zip(s[1:], v_refs[1:]):
            ov = ov + _dot(jnp.exp(t - m).astype(bf16), v[0])
        o_ref[0, :, hd * HEAD_DIM:(hd + 1) * HEAD_DIM] = (
            ov[:, :HEAD_DIM] / ov[:, HEAD_DIM:]).astype(o_ref.dtype)


def _attention(q, ks, vs, tq):
    b, n, _ = q.shape
    nseg = len(ks)
    gw = GQA * HEAD_DIM
    k_specs = [pl.BlockSpec((1, k.shape[1], HEAD_DIM), lambda b, h, i: (b, 0, h)) for k in ks]
    v_specs = [pl.BlockSpec((1, v.shape[1], 2 * HEAD_DIM), lambda b, h, i: (b, 0, h)) for v in vs]
    return pl.pallas_call(
        functools.partial(_attn_kernel, nseg=nseg),
        grid=(b, N_KV, n // tq),
        in_specs=[pl.BlockSpec((1, tq, gw), lambda b, h, i: (b, i, h))] + k_specs + v_specs,
        out_specs=pl.BlockSpec((1, tq, gw), lambda b, h, i: (b, i, h)),
        out_shape=jax.ShapeDtypeStruct((b, n, N_HEADS * HEAD_DIM), bf16),
        compiler_params=_params(("arbitrary", "arbitrary", "arbitrary")),
        name="attn",
    )(q, *ks, *vs)


def _pad2(a, rows, cols):
    return jnp.pad(a, ((0, rows - a.shape[0]), (0, cols - a.shape[1])))


def kernel(x, c, ctx, c_ctx, mod_w, mod_b, norm1_g, norm2_g, ffn_w_gu, ffn_w_down, even_w_in,
           gmlp_ln_g, gmlp_ln_b, gmlp_w_s, gmlp_b_s, hyena_conv_w, hyena_conv_b, hyena_f_w1,
           hyena_f_b1, hyena_f_w2, hyena_f_b2, hyena_f_w3, hyena_freq, hyena_skip, even_w_out,
           attn_w_qkv, attn_q_g, attn_k_g, attn_w_o, final_g):
    lat_row = lambda b: b
    ctx_row = lambda b: CTX_ROW
    streams = {"lat": (SEQ, lat_row, 512), "ctx": (CTX, ctx_row, 512)}

    cond = jnp.zeros((MOD_ROWS, D), f32).at[:NB].set(c).at[CTX_ROW].set(c_ctx)
    mods = _mods(cond, mod_w, mod_b)

    dft = {p: _dft_consts(p) for p in set(HY_P.values())}
    fconst = {n: _filter_consts(n) for n in (SEQ, CTX)}
    rope = _rope_tables(SEQ)
    nq = N_HEADS * HEAD_DIM
    nk = N_KV * HEAD_DIM

    w_gu_all, w_down_all = ffn_w_gu.astype(bf16), ffn_w_down.astype(bf16)
    w_in_all, w_out_all, w_s_all = (even_w_in.astype(bf16), even_w_out.astype(bf16),
                                    gmlp_w_s.astype(bf16))
    w_qkv_all, w_o_all = attn_w_qkv.astype(bf16), attn_w_o.astype(bf16)

    xs = {"lat": x, "ctx": ctx.reshape(1, NB * CTX, D)}
    for layer in range(DEPTH):
        last = layer == DEPTH - 1
        is_even = layer % 2 == 0
        mods3 = mods[layer].reshape(MOD_ROWS, 1, 6 * D)
        g1 = norm1_g[layer].reshape(1, D)
        g2 = norm2_g[layer].reshape(1, D)
        w_gu = (w_gu_all, layer)
        w_down = (w_down_all, layer)
        mixed = {}
        if is_even:
            i = layer // 2
            w_in = (w_in_all, i)
            w_out = (w_out_all, i)
            w_s = (w_s_all, i)
            b_s_rows = jnp.repeat(gmlp_b_s[i].T, CHUNK, axis=1)
            ln_g = gmlp_ln_g[i].reshape(1, D)
            ln_b = gmlp_ln_b[i].reshape(1, D)
            conv_w3 = hyena_conv_w[i].reshape(3, 3, D).transpose(1, 0, 2)
            conv_b3 = hyena_conv_b[i].reshape(3, 1, D)
            w1p = _pad2(hyena_f_w1[i], LANES, LANES)
            w2p = _pad2(hyena_f_w2[i], LANES, LANES)
            b1p = _pad2(hyena_f_b1[i][None, :], 1, LANES)
            b2p = _pad2(hyena_f_b2[i][None, :], 1, LANES)
            f0p = _pad2(hyena_freq[i, 0][None, :], 1, LANES)
            f1p = _pad2(hyena_freq[i, 1][None, :], 1, LANES)
            w3p = jnp.pad(hyena_f_w3[i].reshape(FILTER_HIDDEN, 2, 2 * D).transpose(1, 0, 2),
                          ((0, 0), (0, LANES - FILTER_HIDDEN), (0, 0)))
            for key, (n, row_of, tm) in streams.items():
                if key == "ctx" and last:
                    continue
                xc = xs[key]
                p = HY_P[n]
                *hy_consts, gmat = dft[p]
                zemb, deltas = fconst[n]
                hraw, hsum = _filter_time(n, zemb, deltas, w1p, b1p, w2p, b2p, f0p, f1p, w3p)
                h_a, h_b = _filter_spec(n, p, gmat, hraw, hsum)
                y_a, pb = _even_in(xc, g1, mods3, row_of, w_in, ln_g, ln_b, w_s, b_s_rows, tm)
                z_b = _hyena(pb.reshape(NB, n, 3 * D), conv_w3, conv_b3, hyena_skip[i],
                             h_a, h_b, hy_consts, p)
                mixed[key] = ([y_a, z_b.reshape(y_a.shape)], w_out)
        else:
            j = layer // 2
            w_qkv = (w_qkv_all, j)
            w_o = (w_o_all, j)
            q_g = attn_q_g[j].reshape(1, HEAD_DIM)
            k_g = attn_k_g[j].reshape(1, HEAD_DIM)
            q_l, k_l, v_l = _qkv(xs["lat"], g1, mods3, lat_row, w_qkv, q_g, k_g, rope, True, 1024)
            if last:
                k_c, v_c = _qkv(xs["ctx"], g1, mods3, ctx_row, w_qkv, q_g, k_g, None, False, 512)
            else:
                q_c, k_c, v_c = _qkv(xs["ctx"], g1, mods3, ctx_row, w_qkv, q_g, k_g, None, True, 512)
            k_c = k_c.reshape(NB, CTX, nk)
            v_c = v_c.reshape(NB, CTX, 2 * nk)
            mixed["lat"] = ([_attention(q_l, [k_c, k_l], [v_c, v_l], 1024)], w_o)
            if not last:
                o_c = _attention(q_c.reshape(NB, CTX, nq), [k_c], [v_c], CTX)
                mixed["ctx"] = ([o_c.reshape(1, NB * CTX, nq)], w_o)
        for key, (acts, w_mix) in mixed.items():
            _, row_of, tm = streams[key]
            fg = final_g.reshape(1, D) if last else None
            xs[key] = _mix_ffn(xs[key], mods3, row_of, acts, w_mix, g2, w_gu, w_down, tm, fg)
    return xs["lat"]
```

```python
import functools
import math

import numpy as np
import jax
import jax.numpy as jnp
from jax import lax
from jax.experimental import pallas as pl
from jax.experimental.pallas import tpu as pltpu

D = 1024
NB = 16
SEQ = 2048
CTX = 256
DEPTH = 4
GRID_W = 64
EPS = 1e-6
CHUNK = 128
A_GROUPS = 8
HEAD_DIM = 128
N_HEADS = 8
N_KV = 2
GQA = N_HEADS // N_KV
D_FF = 2816
FILTER_EMB = 33
FILTER_BANDS = 16
FILTER_HIDDEN = 64
MIN_DECAY = math.log(1e-2) / 1.5
MAX_DECAY = math.log(1e-2) / 0.3
ROPE_THETA = 10000.0

LANES = 128
CTX_ROW = NB
MOD_ROWS = 24
HY_P = {SEQ: 1024, CTX: 256}
HY_TC = LANES
HY_NBAT = {SEQ: 2, CTX: 8}
HY_ROWS = 32
FFN_TF = 256
VMEM_LIMIT = 56 * 1024 * 1024

bf16 = jnp.bfloat16
f32 = jnp.float32


_NT = (((1,), (1,)), ((), ()))


def _dot(a, b):
    return jnp.dot(a, b, preferred_element_type=f32)


def _mm_f32(a, b):
    return jnp.dot(a, b, preferred_element_type=f32, precision=lax.Precision.HIGHEST)


def _params(sem):
    return pltpu.CompilerParams(dimension_semantics=sem, vmem_limit_bytes=VMEM_LIMIT)


def _whole(a):
    nd = a.ndim
    return pl.BlockSpec(a.shape, lambda *_: (0,) * nd)


def _layer_spec(w):
    arr, idx = w
    nd = arr.ndim - 1
    return pl.BlockSpec((None,) + arr.shape[1:], lambda *_: (idx,) + (0,) * nd,
                        pipeline_mode=pl.Buffered(1))


def _mods_kernel(cond_ref, w_ref, b_ref, o_ref):
    a = jax.nn.silu(cond_ref[...]).astype(bf16)
    o_ref[0] = _dot(a, w_ref[0].astype(bf16)) + b_ref[0]


def _mods(cond, mod_w, mod_b):
    tn = 1024
    n6 = 6 * D
    return pl.pallas_call(
        _mods_kernel,
        grid=(DEPTH, n6 // tn),
        in_specs=[
            pl.BlockSpec((MOD_ROWS, D), lambda l, j: (0, 0)),
            pl.BlockSpec((1, D, tn), lambda l, j: (l, 0, j)),
            pl.BlockSpec((1, 1, tn), lambda l, j: (l, 0, j)),
        ],
        out_specs=pl.BlockSpec((1, MOD_ROWS, tn), lambda l, j: (l, 0, j)),
        out_shape=jax.ShapeDtypeStruct((DEPTH, MOD_ROWS, n6), f32),
        compiler_params=_params(("arbitrary", "arbitrary")),
        name="mods",
    )(cond, mod_w, mod_b.reshape(DEPTH, 1, n6))


def _mod_spec(row_of, chunk):
    return pl.BlockSpec((1, 1, D), lambda b, i: (row_of(b), 0, chunk))


def _row_spec(tm, width):
    return pl.BlockSpec((1, tm, width), lambda b, i: (b, i, 0))


def _norm_mod(x, g, sc, sh):
    y = x * lax.rsqrt(jnp.mean(x * x, axis=-1, keepdims=True) + EPS)
    return (y * g) * (1.0 + sc) + sh


def _gelu(x):
    return 0.5 * x * (1.0 + lax.erf(x * (2.0 ** -0.5)))


def _even_in_kernel(x_ref, g_ref, sh_ref, sc_ref, w_ref, lng_ref, lnb_ref, ws_ref, bs_ref,
                    ya_ref, pb_ref, *, tm):
    h = _norm_mod(x_ref[0], g_ref[...], sc_ref[0], sh_ref[0]).astype(bf16)
    v = _gelu(_dot(h, w_ref[:, D:2 * D]))
    u = _gelu(_dot(h, w_ref[:, :D]))
    pb = _dot(h, w_ref[:, 2 * D:])
    for t in range(3 * D // HY_TC):
        pb_ref[t, 0] = pb[:, t * HY_TC:(t + 1) * HY_TC]
    mu = jnp.mean(v, axis=-1, keepdims=True)
    vc = v - mu
    var = jnp.mean(vc * vc, axis=-1, keepdims=True)
    vn = (vc * lax.rsqrt(var + 1e-5) * lng_ref[...] + lnb_ref[...]).astype(bf16)
    nck = tm // CHUNK
    for g in range(A_GROUPS):
        cols = slice(g * CHUNK, (g + 1) * CHUNK)
        rhs = jnp.concatenate([vn[k * CHUNK:(k + 1) * CHUNK, cols] for k in range(nck)], axis=1)
        sv = _dot(ws_ref[g], rhs)
        for k in range(nck):
            rows = slice(k * CHUNK, (k + 1) * CHUNK)
            gate = sv[:, k * CHUNK:(k + 1) * CHUNK] + bs_ref[:, cols]
            ya_ref[0, rows, cols] = (u[rows, cols] * gate).astype(bf16)


def _even_in(x, g, mods3, row_of, w_in, ln_g, ln_b, w_s, b_s_rows, tm):
    b, n, _ = x.shape
    return pl.pallas_call(
        functools.partial(_even_in_kernel, tm=tm),
        grid=(b, n // tm),
        in_specs=[
            _row_spec(tm, D), _whole(g), _mod_spec(row_of, 0), _mod_spec(row_of, 1),
            _layer_spec(w_in), _whole(ln_g), _whole(ln_b), _layer_spec(w_s), _whole(b_s_rows),
        ],
        out_specs=[_row_spec(tm, D),
                   pl.BlockSpec((3 * D // HY_TC, 1, tm, HY_TC), lambda b, i: (0, b, i, 0))],
        out_shape=[jax.ShapeDtypeStruct((b, n, D), bf16),
                   jax.ShapeDtypeStruct((3 * D // HY_TC, b, n, HY_TC), f32)],
        compiler_params=_params(("arbitrary", "arbitrary")),
        name="even_in",
    )(x, g, mods3, mods3, w_in[0], ln_g, ln_b, w_s[0], b_s_rows)


def _filter_consts(n):
    lag = np.abs(np.arange(2 * n) - n)
    lag[0] = 0
    t = np.linspace(0.0, 1.0, n)[lag][:, None]
    w = (2.0 * math.pi / n) * lag[:, None]
    bands = np.linspace(1e-4, FILTER_BANDS - 1, FILTER_BANDS)[None, :]
    ang = bands * w
    z = np.concatenate([t, np.cos(ang), -np.sin(ang)], axis=-1)
    zp = np.zeros((2 * n, LANES), np.float64)
    zp[:, :FILTER_EMB] = z
    deltas = np.abs(np.linspace(MIN_DECAY, MAX_DECAY, D))
    return jnp.asarray(zp, f32), jnp.asarray(np.tile(deltas, 2)[None, :], f32)


def _filter_time_kernel(z_ref, w1_ref, b1_ref, w2_ref, b2_ref, f0_ref, f1_ref, w3_ref, dl_ref,
                        h_ref, s_ref, *, tr):
    r = pl.program_id(0)
    z = z_ref[...]
    hid = jnp.sin(f0_ref[...] * (_mm_f32(z, w1_ref[...]) + b1_ref[...]))
    hid = jnp.sin(f1_ref[...] * (_mm_f32(hid, w2_ref[...]) + b2_ref[...]))
    h = _mm_f32(hid, w3_ref[0])
    h = h * jnp.exp(-z[:, 0:1] * dl_ref[...])
    row = lax.broadcasted_iota(jnp.int32, h.shape, 0) + r * tr
    h = jnp.where(row == 0, 0.0, h)
    h_ref[...] = h

    @pl.when(r == 0)
    def _():
        s_ref[...] = jnp.zeros_like(s_ref)

    s_ref[...] += jnp.sum(jnp.abs(h), axis=0, keepdims=True)


def _filter_time(n, zemb, deltas, w1p, b1p, w2p, b2p, f0p, f1p, w3p):
    tr = 256
    half = n // tr
    c2 = 2 * D
    return pl.pallas_call(
        functools.partial(_filter_time_kernel, tr=tr),
        grid=(2 * n // tr,),
        in_specs=[
            pl.BlockSpec((tr, LANES), lambda r: (r, 0)),
            _whole(w1p), _whole(b1p), _whole(w2p), _whole(b2p), _whole(f0p), _whole(f1p),
            pl.BlockSpec((1, LANES, c2), lambda r: (jnp.where(r < half, 1, 0), 0, 0)),
            _whole(deltas),
        ],
        out_specs=[pl.BlockSpec((tr, c2), lambda r: (r, 0)), pl.BlockSpec((1, c2), lambda r: (0, 0))],
        out_shape=[jax.ShapeDtypeStruct((2 * n, c2), f32), jax.ShapeDtypeStruct((1, c2), f32)],
        compiler_params=_params(("arbitrary",)),
        name="filter_time",
    )(zemb, w1p, b1p, w2p, b2p, f0p, f1p, w3p, deltas)


def _split_bf16(a):
    hi = a.astype(bf16)
    return hi, (a - hi.astype(f32)).astype(bf16)


def _dft_consts(p):
    q = p // 2
    theta = np.pi * (2 * np.arange(q) + 1) / (2 * p)
    te = theta[:, None] * (2 * np.arange(q))[None, :]
    to = theta[:, None] * (2 * np.arange(q) + 1)[None, :]
    fe = np.concatenate([np.cos(te), -np.sin(te)], axis=0)
    fo = np.concatenate([np.cos(to), -np.sin(to)], axis=0)
    ie = np.concatenate([np.cos(te).T, -np.sin(te).T], axis=1) / p
    io = np.concatenate([np.cos(to).T, -np.sin(to).T], axis=1) / p
    freq = np.concatenate([np.arange(q), p - 1 - np.arange(q)])
    tf = (np.pi * (2 * freq + 1) / (2 * p))[:, None] * np.arange(-p, p)[None, :]
    g = np.concatenate([np.cos(tf), -np.sin(tf)], axis=0)
    g[:, 0] = 0.0
    return tuple(jnp.asarray(a, bf16) for a in (fe, fo, ie, io, g))


def _filter_spec_kernel(g_ref, lo_ref, hi_ref, s_ref, ha_ref, hb_ref, *, p):
    h = (_dot(g_ref[:, :p], lo_ref[...].astype(bf16))
         + _dot(g_ref[:, p:], hi_ref[...].astype(bf16)))
    h = h / s_ref[...]
    ha_ref[0] = h[:p]
    hb_ref[0] = h[p:]


def _filter_spec(n, p, gmat, hraw, hsum):
    nb = n // p
    c2 = 2 * D
    tc = 512
    nd = 2 * nb - 1
    out = jax.ShapeDtypeStruct((nd, p, c2), f32)
    return pl.pallas_call(
        functools.partial(_filter_spec_kernel, p=p),
        grid=(nd, c2 // tc),
        in_specs=[
            _whole(gmat),
            pl.BlockSpec((p, tc), lambda d, c: (d, c)),
            pl.BlockSpec((p, tc), lambda d, c: (d + 1, c)),
            pl.BlockSpec((1, tc), lambda d, c: (0, c)),
        ],
        out_specs=[pl.BlockSpec((1, p, tc), lambda d, c: (d, 0, c))] * 2,
        out_shape=[out, out],
        compiler_params=_params(("arbitrary", "arbitrary")),
        name="filter_spec",
    )(gmat, hraw, hraw, hsum)


def _hyena_kernel(pv_ref, p1_ref, p2_ref, cw_ref, cb_ref, skip_ref,
                  ha0_ref, hb0_ref, ha1_ref, hb1_ref, fe_ref, fo_ref, ie_ref, io_ref,
                  o_ref, sv_ref, s1_ref, s2_ref, u_ref, za_ref, zb_ref, ye_ref, yo_ref, *, n, p):
    nb = n // p
    q = p // 2
    tc = HY_TC
    nbat = o_ref.shape[0]
    halves = [slice(bb * tc, (bb + 1) * tc) for bb in range(nbat)]

    def short_conv(p_ref, part, dst_ref):
        w = cw_ref[part]
        w0, w1, w2, bias = w[0:1], w[1:2], w[2:3], cb_ref[part]
        edge = lax.broadcasted_iota(jnp.int32, (8, tc), 0)
        for bb in range(nbat):
            dst_ref[bb, 8:n - 8, :] = (p_ref[bb, 7:n - 9, :] * w0 + p_ref[bb, 8:n - 8, :] * w1
                                       + p_ref[bb, 9:n - 7, :] * w2 + bias)
            top = p_ref[bb, 0:8, :]
            prev = jnp.where(edge == 0, 0.0, pltpu.roll(top, 1, axis=0))
            dst_ref[bb, 0:8, :] = prev * w0 + top * w1 + p_ref[bb, 1:9, :] * w2 + bias
            bot = p_ref[bb, n - 8:n, :]
            nxt = jnp.where(edge == 7, 0.0, pltpu.roll(bot, 7, axis=0))
            dst_ref[bb, n - 8:n, :] = p_ref[bb, n - 9:n - 1, :] * w0 + bot * w1 + nxt * w2 + bias

    def samples(ref, start):
        rows = pl.ds(start, q, stride=2)
        return jnp.concatenate([ref.at[bb][rows, :] for bb in range(nbat)], axis=1)

    def forward(src_ref):
        for j in range(nb):
            e = _dot(fe_ref[...], samples(src_ref, j * p).astype(bf16))
            o = _dot(fo_ref[...], samples(src_ref, j * p + 1).astype(bf16))
            za_ref[j, 0:q] = e[:q] + o[:q]
            za_ref[j, q:p] = e[:q] - o[:q]
            zb_ref[j, 0:q] = e[q:] + o[q:]
            zb_ref[j, q:p] = o[q:] - e[q:]

    def filter_and_gate(src_ref, gate_ref, ha_ref, hb_ref, skip, dst_ref):
        for i in range(nb):
            for c in range(q // HY_ROWS):
                lo = slice(c * HY_ROWS, (c + 1) * HY_ROWS)
                hi = slice(q + c * HY_ROWS, q + (c + 1) * HY_ROWS)
                acc = []
                for rows in (lo, hi):
                    ya = [None] * nbat
                    yb = [None] * nbat
                    for j in range(nb):
                        d = i - j + (nb - 1)
                        h_a = ha_ref[d, rows, :]
                        h_b = hb_ref[d, rows, :]
                        for bb, lanes in enumerate(halves):
                            z_a = za_ref[j, rows, lanes]
                            z_b = zb_ref[j, rows, lanes]
                            t_a = h_a * z_a - h_b * z_b
                            t_b = h_a * z_b + h_b * z_a
                            ya[bb] = t_a if ya[bb] is None else ya[bb] + t_a
                            yb[bb] = t_b if yb[bb] is None else yb[bb] + t_b
                    acc.append((ya, yb))
                (ya_lo, yb_lo), (ya_hi, yb_hi) = acc
                for bb, lanes in enumerate(halves):
                    ye_ref[i, lo, lanes] = (ya_lo[bb] + ya_hi[bb]).astype(bf16)
                    ye_ref[i, hi, lanes] = (yb_lo[bb] - yb_hi[bb]).astype(bf16)
                    yo_ref[i, lo, lanes] = (ya_lo[bb] - ya_hi[bb]).astype(bf16)
                    yo_ref[i, hi, lanes] = (yb_lo[bb] + yb_hi[bb]).astype(bf16)
            for parity, y in ((0, _dot(ie_ref[...], ye_ref[i])), (1, _dot(io_ref[...], yo_ref[i]))):
                rows = pl.ds(i * p + parity, q, stride=2)
                for bb, lanes in enumerate(halves):
                    dst_ref.at[bb][rows, :] = gate_ref.at[bb][rows, :] * (
                        y[:, lanes] + src_ref.at[bb][rows, :] * skip)

    short_conv(pv_ref, 0, sv_ref)
    forward(sv_ref)
    short_conv(p1_ref, 1, s1_ref)
    short_conv(p2_ref, 2, s2_ref)
    filter_and_gate(sv_ref, s1_ref, ha0_ref, hb0_ref, skip_ref[0:1, :], u_ref)
    forward(u_ref)
    filter_and_gate(u_ref, s2_ref, ha1_ref, hb1_ref, skip_ref[1:2, :], sv_ref)
    for bb in range(nbat):
        o_ref[bb] = sv_ref[bb].astype(o_ref.dtype)


def _hyena(pb, conv_w3, conv_b3, skip, h_a, h_b, consts, p):
    _, b, n, tc = pb.shape
    nct = D // tc
    nb = n // p
    nd = 2 * nb - 1
    nbat = HY_NBAT[n]
    wide = nbat * tc

    def pspec(part):
        return pl.BlockSpec((None, nbat, n, tc), lambda c, b: (part * nct + c, b, 0, 0))

    def hspec(order):
        return pl.BlockSpec((nd, p, tc), lambda c, b: (0, 0, order * nct + c))

    return pl.pallas_call(
        functools.partial(_hyena_kernel, n=n, p=p),
        grid=(nct, b // nbat),
        in_specs=[
            pspec(0), pspec(1), pspec(2),
            pl.BlockSpec((3, 3, tc), lambda c, b: (0, 0, c)),
            pl.BlockSpec((3, 1, tc), lambda c, b: (0, 0, c)),
            pl.BlockSpec((2, tc), lambda c, b: (0, c)),
            hspec(0), hspec(0), hspec(1), hspec(1),
        ] + [_whole(a) for a in consts],
        out_specs=pl.BlockSpec((nbat, n, tc), lambda c, b: (b, 0, c)),
        out_shape=jax.ShapeDtypeStruct((b, n, D), bf16),
        scratch_shapes=(
            [pltpu.VMEM((nbat, n, tc), f32)] * 4
            + [pltpu.VMEM((nb, p, wide), f32)] * 2
            + [pltpu.VMEM((nb, p, wide), bf16)] * 2
        ),
        compiler_params=_params(("arbitrary", "arbitrary")),
        name="hyena",
    )(pb, pb, pb, conv_w3, conv_b3, skip, h_a, h_b, h_a, h_b, *consts)


def _mix_ffn_kernel(*refs, na, final):
    x_ref, gate1_ref = refs[:2]
    a_refs = refs[2:2 + na]
    wo_ref, g_ref, sh_ref, sc_ref, gate2_ref, wgu_ref, wd_ref = refs[2 + na:9 + na]
    rest = refs[9 + na:]
    fg_ref = rest[0] if final else None
    o_ref, x1_ref = rest[-2:]

    mix = _dot(a_refs[0][0], wo_ref[0:D, :])
    for k in range(1, na):
        mix = mix + _dot(a_refs[k][0], wo_ref[k * D:(k + 1) * D, :])
    x1_ref[...] = x_ref[0] + gate1_ref[0] * mix

    h = _norm_mod(x1_ref[...], g_ref[...], sc_ref[0], sh_ref[0]).astype(bf16)
    acc = None
    for f in range(D_FF // FFN_TF):
        cols = slice(f * FFN_TF, (f + 1) * FFN_TF)
        ucols = slice(D_FF + f * FFN_TF, D_FF + (f + 1) * FFN_TF)
        act = (jax.nn.silu(_dot(h, wgu_ref[:, cols])) * _dot(h, wgu_ref[:, ucols])).astype(bf16)
        part = _dot(act, wd_ref[cols, :])
        acc = part if acc is None else acc + part
    y = x1_ref[...] + gate2_ref[0] * acc
    if final:
        y = y * lax.rsqrt(jnp.mean(y * y, axis=-1, keepdims=True) + EPS) * fg_ref[...]
    o_ref[0] = y


def _mix_ffn(x, mods3, row_of, acts, w_o, g, w_gu, w_down, tm, final_g=None):
    b, n, _ = x.shape
    na = len(acts)
    assert w_o[0].shape[1:] == (na * D, D) and all(a.shape[-1] == D for a in acts)
    extra = [] if final_g is None else [final_g]
    return pl.pallas_call(
        functools.partial(_mix_ffn_kernel, na=na, final=final_g is not None),
        grid=(b, n // tm),
        in_specs=([_row_spec(tm, D), _mod_spec(row_of, 2)]
                  + [_row_spec(tm, D) for _ in acts]
                  + [_layer_spec(w_o), _whole(g), _mod_spec(row_of, 3), _mod_spec(row_of, 4),
                     _mod_spec(row_of, 5), _layer_spec(w_gu), _layer_spec(w_down)]
                  + [_whole(a) for a in extra]),
        out_specs=_row_spec(tm, D),
        out_shape=jax.ShapeDtypeStruct((b, n, D), f32),
        scratch_shapes=[pltpu.VMEM((tm, D), f32)],
        compiler_params=_params(("arbitrary", "arbitrary")),
        name="mix_ffn",
    )(x, mods3, *acts, w_o[0], g, mods3, mods3, mods3, w_gu[0], w_down[0], *extra)


def _rope_tables(n):
    rows = n // GRID_W
    row = np.repeat(np.arange(rows), GRID_W).astype(np.float64)
    col = np.tile(np.arange(GRID_W), rows).astype(np.float64)
    half = HEAD_DIM // 2
    inv = ROPE_THETA ** (-np.arange(0, half, 2, dtype=np.float64) / half)
    ang = np.concatenate([row[:, None] * inv, col[:, None] * inv], axis=-1)
    cos, sin = np.cos(ang), np.sin(ang)
    return (jnp.asarray(np.concatenate([cos, cos], axis=-1), f32),
            jnp.asarray(np.concatenate([-sin, sin], axis=-1), f32))


def _qkv_kernel(*refs, with_q, rope):
    x_ref, g_ref, sh_ref, sc_ref, w_ref, qg_ref, kg_ref = refs[:7]
    pos = 7
    if rope:
        cos_ref, sin_ref = refs[7:9]
        pos = 9
    out_refs = refs[pos:]
    h = _norm_mod(x_ref[0], g_ref[...], sc_ref[0], sh_ref[0]).astype(bf16)
    qkv = _dot(h, w_ref[...] if with_q else w_ref[:, N_HEADS * HEAD_DIM:])
    ones = jnp.ones((HEAD_DIM, HEAD_DIM), bf16)

    def head(t, gain, scale):
        sq_hi, sq_lo = _split_bf16(t * t)
        ssq = _dot(sq_hi, ones) + _dot(sq_lo, ones)
        t = t * lax.rsqrt(ssq * (1.0 / HEAD_DIM) + EPS) * gain
        if rope:
            t = t * cos_ref[...] + pltpu.roll(t, HEAD_DIM // 2, axis=1) * sin_ref[...]
        return (t * scale).astype(bf16)

    off = 0
    if with_q:
        q_ref, k_ref, v_ref = out_refs
        for hd in range(N_HEADS):
            cols = slice(hd * HEAD_DIM, (hd + 1) * HEAD_DIM)
            q_ref[0, :, cols] = head(qkv[:, cols], qg_ref[...], HEAD_DIM ** -0.5)
        off = N_HEADS * HEAD_DIM
    else:
        k_ref, v_ref = out_refs
    for kv in range(N_KV):
        cols = slice(kv * HEAD_DIM, (kv + 1) * HEAD_DIM)
        src = slice(off + kv * HEAD_DIM, off + (kv + 1) * HEAD_DIM)
        k_ref[0, :, cols] = head(qkv[:, src], kg_ref[...], 1.0)
    voff = off + N_KV * HEAD_DIM
    for kv in range(N_KV):
        src = slice(voff + kv * HEAD_DIM, voff + (kv + 1) * HEAD_DIM)
        v_ref[0, :, 2 * kv * HEAD_DIM:(2 * kv + 1) * HEAD_DIM] = qkv[:, src].astype(bf16)
        v_ref[0, :, (2 * kv + 1) * HEAD_DIM:(2 * kv + 2) * HEAD_DIM] = jnp.ones(
            (qkv.shape[0], HEAD_DIM), bf16)


def _qkv(x, g, mods3, row_of, w, q_g, k_g, tables, with_q, tm):
    b, n, _ = x.shape
    nq = N_HEADS * HEAD_DIM
    nk = N_KV * HEAD_DIM
    rope = tables is not None
    in_specs = [_row_spec(tm, D), _whole(g), _mod_spec(row_of, 0), _mod_spec(row_of, 1),
                _layer_spec(w), _whole(q_g), _whole(k_g)]
    args = [x, g, mods3, mods3, w[0], q_g, k_g]
    if rope:
        in_specs += [pl.BlockSpec((tm, HEAD_DIM), lambda b, i: (i, 0))] * 2
        args += list(tables)
    out_specs = [_row_spec(tm, nk), _row_spec(tm, 2 * nk)]
    out_shape = [jax.ShapeDtypeStruct((b, n, nk), bf16), jax.ShapeDtypeStruct((b, n, 2 * nk), bf16)]
    if with_q:
        out_specs = [_row_spec(tm, nq)] + out_specs
        out_shape = [jax.ShapeDtypeStruct((b, n, nq), bf16)] + out_shape
    return pl.pallas_call(
        functools.partial(_qkv_kernel, with_q=with_q, rope=rope),
        grid=(b, n // tm),
        in_specs=in_specs,
        out_specs=out_specs,
        out_shape=out_shape,
        compiler_params=_params(("arbitrary", "arbitrary")),
        name="qkv",
    )(*args)


def _attn_kernel(*refs, nseg):
    q_ref = refs[0]
    k_refs = refs[1:1 + nseg]
    v_refs = refs[1 + nseg:1 + 2 * nseg]
    o_ref = refs[1 + 2 * nseg]
    def scores(hd):
        q = q_ref[0, :, hd * HEAD_DIM:(hd + 1) * HEAD_DIM]
        return [lax.dot_general(q, k[0], _NT, preferred_element_type=f32) for k in k_refs]

    s_next = scores(0)
    for hd in range(GQA):
        s = s_next
        if hd + 1 < GQA:
            s_next = scores(hd + 1)
        m = jnp.max(s[0], axis=-1, keepdims=True)
        for t in s[1:]:
            m = jnp.maximum(m, jnp.max(t, axis=-1, keepdims=True))
        ov = _dot(jnp.exp(s[0] - m).astype(bf16), v_refs[0][0])
        for t, v in zip(s[1:], v_refs[1:]):
            ov = ov + _dot(jnp.exp(t - m).astype(bf16), v[0])
        o_ref[0, :, hd * HEAD_DIM:(hd + 1) * HEAD_DIM] = (
            ov[:, :HEAD_DIM] / ov[:, HEAD_DIM:]).astype(o_ref.dtype)


def _attention(q, ks, vs, tq):
    b, n, _ = q.shape
    nseg = len(ks)
    gw = GQA * HEAD_DIM
    k_specs = [pl.BlockSpec((1, k.shape[1], HEAD_DIM), lambda b, h, i: (b, 0, h)) for k in ks]
    v_specs = [pl.BlockSpec((1, v.shape[1], 2 * HEAD_DIM), lambda b, h, i: (b, 0, h)) for v in vs]
    return pl.pallas_call(
        functools.partial(_attn_kernel, nseg=nseg),
        grid=(b, N_KV, n // tq),
        in_specs=[pl.BlockSpec((1, tq, gw), lambda b, h, i: (b, i, h))] + k_specs + v_specs,
        out_specs=pl.BlockSpec((1, tq, gw), lambda b, h, i: (b, i, h)),
        out_shape=jax.ShapeDtypeStruct((b, n, N_HEADS * HEAD_DIM), bf16),
        compiler_params=_params(("arbitrary", "arbitrary", "arbitrary")),
        name="attn",
    )(q, *ks, *vs)


def _pad2(a, rows, cols):
    return jnp.pad(a, ((0, rows - a.shape[0]), (0, cols - a.shape[1])))


def kernel(x, c, ctx, c_ctx, mod_w, mod_b, norm1_g, norm2_g, ffn_w_gu, ffn_w_down, even_w_in,
           gmlp_ln_g, gmlp_ln_b, gmlp_w_s, gmlp_b_s, hyena_conv_w, hyena_conv_b, hyena_f_w1,
           hyena_f_b1, hyena_f_w2, hyena_f_b2, hyena_f_w3, hyena_freq, hyena_skip, even_w_out,
           attn_w_qkv, attn_q_g, attn_k_g, attn_w_o, final_g):
    lat_row = lambda b: b
    ctx_row = lambda b: CTX_ROW
    streams = {"lat": (SEQ, lat_row, 512), "ctx": (CTX, ctx_row, 512)}

    cond = jnp.zeros((MOD_ROWS, D), f32).at[:NB].set(c).at[CTX_ROW].set(c_ctx)
    mods = _mods(cond, mod_w, mod_b)

    dft = {p: _dft_consts(p) for p in set(HY_P.values())}
    fconst = {n: _filter_consts(n) for n in (SEQ, CTX)}
    rope = _rope_tables(SEQ)
    nq = N_HEADS * HEAD_DIM
    nk = N_KV * HEAD_DIM

    w_gu_all, w_down_all = ffn_w_gu.astype(bf16), ffn_w_down.astype(bf16)
    w_in_all, w_out_all, w_s_all = (even_w_in.astype(bf16), even_w_out.astype(bf16),
                                    gmlp_w_s.astype(bf16))
    w_qkv_all, w_o_all = attn_w_qkv.astype(bf16), attn_w_o.astype(bf16)

    xs = {"lat": x, "ctx": ctx.reshape(1, NB * CTX, D)}
    for layer in range(DEPTH):
        last = layer == DEPTH - 1
        is_even = layer % 2 == 0
        mods3 = mods[layer].reshape(MOD_ROWS, 1, 6 * D)
        g1 = norm1_g[layer].reshape(1, D)
        g2 = norm2_g[layer].reshape(1, D)
        w_gu = (w_gu_all, layer)
        w_down = (w_down_all, layer)
        mixed = {}
        if is_even:
            i = layer // 2
            w_in = (w_in_all, i)
            w_out = (w_out_all, i)
            w_s = (w_s_all, i)
            b_s_rows = jnp.repeat(gmlp_b_s[i].T, CHUNK, axis=1)
            ln_g = gmlp_ln_g[i].reshape(1, D)
            ln_b = gmlp_ln_b[i].reshape(1, D)
            conv_w3 = hyena_conv_w[i].reshape(3, 3, D).transpose(1, 0, 2)
            conv_b3 = hyena_conv_b[i].reshape(3, 1, D)
            w1p = _pad2(hyena_f_w1[i], LANES, LANES)
            w2p = _pad2(hyena_f_w2[i], LANES, LANES)
            b1p = _pad2(hyena_f_b1[i][None, :], 1, LANES)
            b2p = _pad2(hyena_f_b2[i][None, :], 1, LANES)
            f0p = _pad2(hyena_freq[i, 0][None, :], 1, LANES)
            f1p = _pad2(hyena_freq[i, 1][None, :], 1, LANES)
            w3p = jnp.pad(hyena_f_w3[i].reshape(FILTER_HIDDEN, 2, 2 * D).transpose(1, 0, 2),
                          ((0, 0), (0, LANES - FILTER_HIDDEN), (0, 0)))
            for key, (n, row_of, tm) in streams.items():
                if key == "ctx" and last:
                    continue
                xc = xs[key]
                p = HY_P[n]
                *hy_consts, gmat = dft[p]
                zemb, deltas = fconst[n]
                hraw, hsum = _filter_time(n, zemb, deltas, w1p, b1p, w2p, b2p, f0p, f1p, w3p)
                h_a, h_b = _filter_spec(n, p, gmat, hraw, hsum)
                y_a, pb = _even_in(xc, g1, mods3, row_of, w_in, ln_g, ln_b, w_s, b_s_rows, tm)
                z_b = _hyena(pb.reshape(3 * D // HY_TC, NB, n, HY_TC), conv_w3, conv_b3, hyena_skip[i],
                             h_a, h_b, hy_consts, p)
                mixed[key] = ([y_a, z_b.reshape(y_a.shape)], w_out)
        else:
            j = layer // 2
            w_qkv = (w_qkv_all, j)
            w_o = (w_o_all, j)
            q_g = attn_q_g[j].reshape(1, HEAD_DIM)
            k_g = attn_k_g[j].reshape(1, HEAD_DIM)
            q_l, k_l, v_l = _qkv(xs["lat"], g1, mods3, lat_row, w_qkv, q_g, k_g, rope, True, 1024)
            if last:
                k_c, v_c = _qkv(xs["ctx"], g1, mods3, ctx_row, w_qkv, q_g, k_g, None, False, 512)
            else:
                q_c, k_c, v_c = _qkv(xs["ctx"], g1, mods3, ctx_row, w_qkv, q_g, k_g, None, True, 512)
            k_c = k_c.reshape(NB, CTX, nk)
            v_c = v_c.reshape(NB, CTX, 2 * nk)
            mixed["lat"] = ([_attention(q_l, [k_c, k_l], [v_c, v_l], 1024)], w_o)
            if not last:
                o_c = _attention(q_c.reshape(NB, CTX, nq), [k_c], [v_c], CTX)
                mixed["ctx"] = ([o_c.reshape(1, NB * CTX, nq)], w_o)
        for key, (acts, w_mix) in mixed.items():
            _, row_of, tm = streams[key]
            fg = final_g.reshape(1, D) if last else None
            xs[key] = _mix_ffn(xs[key], mods3, row_of, acts, w_mix, g2, w_gu, w_down, tm, fg)
    return xs["lat"]
```

```python
import functools
import math

import numpy as np
import jax
import jax.numpy as jnp
from jax import lax
from jax.experimental import pallas as pl
from jax.experimental.pallas import tpu as pltpu

D = 1024
NB = 16
SEQ = 2048
CTX = 256
DEPTH = 4
GRID_W = 64
EPS = 1e-6
CHUNK = 128
A_GROUPS = 8
HEAD_DIM = 128
N_HEADS = 8
N_KV = 2
GQA = N_HEADS // N_KV
D_FF = 2816
FILTER_EMB = 33
FILTER_BANDS = 16
FILTER_HIDDEN = 64
MIN_DECAY = math.log(1e-2) / 1.5
MAX_DECAY = math.log(1e-2) / 0.3
ROPE_THETA = 10000.0

LANES = 128
CTX_ROW = NB
MOD_ROWS = 24
HY_P = {SEQ: 1024, CTX: 256}
HY_TC = LANES
HY_NBAT = {SEQ: 2, CTX: 8}
HY_ROWS = 32
HY_RING = 3
FFN_TF = 256
VMEM_LIMIT = 56 * 1024 * 1024

bf16 = jnp.bfloat16
f32 = jnp.float32


_NT = (((1,), (1,)), ((), ()))


def _dot(a, b):
    return jnp.dot(a, b, preferred_element_type=f32)


def _mm_f32(a, b):
    return jnp.dot(a, b, preferred_element_type=f32, precision=lax.Precision.HIGHEST)


def _params(sem):
    return pltpu.CompilerParams(dimension_semantics=sem, vmem_limit_bytes=VMEM_LIMIT)


def _whole(a):
    nd = a.ndim
    return pl.BlockSpec(a.shape, lambda *_: (0,) * nd)


def _layer_spec(w):
    arr, idx = w
    nd = arr.ndim - 1
    return pl.BlockSpec((None,) + arr.shape[1:], lambda *_: (idx,) + (0,) * nd,
                        pipeline_mode=pl.Buffered(1))


def _mods_kernel(cond_ref, w_ref, b_ref, o_ref):
    a = jax.nn.silu(cond_ref[...]).astype(bf16)
    o_ref[0] = _dot(a, w_ref[0].astype(bf16)) + b_ref[0]


def _mods(cond, mod_w, mod_b):
    tn = 1024
    n6 = 6 * D
    return pl.pallas_call(
        _mods_kernel,
        grid=(DEPTH, n6 // tn),
        in_specs=[
            pl.BlockSpec((MOD_ROWS, D), lambda l, j: (0, 0)),
            pl.BlockSpec((1, D, tn), lambda l, j: (l, 0, j)),
            pl.BlockSpec((1, 1, tn), lambda l, j: (l, 0, j)),
        ],
        out_specs=pl.BlockSpec((1, MOD_ROWS, tn), lambda l, j: (l, 0, j)),
        out_shape=jax.ShapeDtypeStruct((DEPTH, MOD_ROWS, n6), f32),
        compiler_params=_params(("arbitrary", "arbitrary")),
        name="mods",
    )(cond, mod_w, mod_b.reshape(DEPTH, 1, n6))


def _mod_spec(row_of, chunk):
    return pl.BlockSpec((1, 1, D), lambda b, i: (row_of(b), 0, chunk))


def _row_spec(tm, width):
    return pl.BlockSpec((1, tm, width), lambda b, i: (b, i, 0))


def _norm_mod(x, g, sc, sh):
    y = x * lax.rsqrt(jnp.mean(x * x, axis=-1, keepdims=True) + EPS)
    return (y * g) * (1.0 + sc) + sh


def _gelu(x):
    return 0.5 * x * (1.0 + lax.erf(x * (2.0 ** -0.5)))


def _even_in_kernel(x_ref, g_ref, sh_ref, sc_ref, w_ref, lng_ref, lnb_ref, ws_ref, bs_ref,
                    ya_ref, pb_ref, *, tm):
    h = _norm_mod(x_ref[0], g_ref[...], sc_ref[0], sh_ref[0]).astype(bf16)
    v = _gelu(_dot(h, w_ref[:, D:2 * D]))
    u = _gelu(_dot(h, w_ref[:, :D]))
    pb_ref[0] = _dot(h, w_ref[:, 2 * D:])
    mu = jnp.mean(v, axis=-1, keepdims=True)
    vc = v - mu
    var = jnp.mean(vc * vc, axis=-1, keepdims=True)
    vn = (vc * lax.rsqrt(var + 1e-5) * lng_ref[...] + lnb_ref[...]).astype(bf16)
    nck = tm // CHUNK
    for g in range(A_GROUPS):
        cols = slice(g * CHUNK, (g + 1) * CHUNK)
        rhs = jnp.concatenate([vn[k * CHUNK:(k + 1) * CHUNK, cols] for k in range(nck)], axis=1)
        sv = _dot(ws_ref[g], rhs)
        for k in range(nck):
            rows = slice(k * CHUNK, (k + 1) * CHUNK)
            gate = sv[:, k * CHUNK:(k + 1) * CHUNK] + bs_ref[:, cols]
            ya_ref[0, rows, cols] = (u[rows, cols] * gate).astype(bf16)


def _even_in(x, g, mods3, row_of, w_in, ln_g, ln_b, w_s, b_s_rows, tm):
    b, n, _ = x.shape
    return pl.pallas_call(
        functools.partial(_even_in_kernel, tm=tm),
        grid=(b, n // tm),
        in_specs=[
            _row_spec(tm, D), _whole(g), _mod_spec(row_of, 0), _mod_spec(row_of, 1),
            _layer_spec(w_in), _whole(ln_g), _whole(ln_b), _layer_spec(w_s), _whole(b_s_rows),
        ],
        out_specs=[_row_spec(tm, D), _row_spec(tm, 3 * D)],
        out_shape=[jax.ShapeDtypeStruct((b, n, D), bf16), jax.ShapeDtypeStruct((b, n, 3 * D), f32)],
        compiler_params=_params(("arbitrary", "arbitrary")),
        name="even_in",
    )(x, g, mods3, mods3, w_in[0], ln_g, ln_b, w_s[0], b_s_rows)


def _filter_consts(n):
    lag = np.abs(np.arange(2 * n) - n)
    lag[0] = 0
    t = np.linspace(0.0, 1.0, n)[lag][:, None]
    w = (2.0 * math.pi / n) * lag[:, None]
    bands = np.linspace(1e-4, FILTER_BANDS - 1, FILTER_BANDS)[None, :]
    ang = bands * w
    z = np.concatenate([t, np.cos(ang), -np.sin(ang)], axis=-1)
    zp = np.zeros((2 * n, LANES), np.float64)
    zp[:, :FILTER_EMB] = z
    deltas = np.abs(np.linspace(MIN_DECAY, MAX_DECAY, D))
    return jnp.asarray(zp, f32), jnp.asarray(np.tile(deltas, 2)[None, :], f32)


def _filter_time_kernel(z_ref, w1_ref, b1_ref, w2_ref, b2_ref, f0_ref, f1_ref, w3_ref, dl_ref,
                        h_ref, s_ref, *, tr):
    r = pl.program_id(0)
    z = z_ref[...]
    hid = jnp.sin(f0_ref[...] * (_mm_f32(z, w1_ref[...]) + b1_ref[...]))
    hid = jnp.sin(f1_ref[...] * (_mm_f32(hid, w2_ref[...]) + b2_ref[...]))
    h = _mm_f32(hid, w3_ref[0])
    h = h * jnp.exp(-z[:, 0:1] * dl_ref[...])
    row = lax.broadcasted_iota(jnp.int32, h.shape, 0) + r * tr
    h = jnp.where(row == 0, 0.0, h)
    h_ref[...] = h

    @pl.when(r == 0)
    def _():
        s_ref[...] = jnp.zeros_like(s_ref)

    s_ref[...] += jnp.sum(jnp.abs(h), axis=0, keepdims=True)


def _filter_time(n, zemb, deltas, w1p, b1p, w2p, b2p, f0p, f1p, w3p):
    tr = 256
    half = n // tr
    c2 = 2 * D
    return pl.pallas_call(
        functools.partial(_filter_time_kernel, tr=tr),
        grid=(2 * n // tr,),
        in_specs=[
            pl.BlockSpec((tr, LANES), lambda r: (r, 0)),
            _whole(w1p), _whole(b1p), _whole(w2p), _whole(b2p), _whole(f0p), _whole(f1p),
            pl.BlockSpec((1, LANES, c2), lambda r: (jnp.where(r < half, 1, 0), 0, 0)),
            _whole(deltas),
        ],
        out_specs=[pl.BlockSpec((tr, c2), lambda r: (r, 0)), pl.BlockSpec((1, c2), lambda r: (0, 0))],
        out_shape=[jax.ShapeDtypeStruct((2 * n, c2), f32), jax.ShapeDtypeStruct((1, c2), f32)],
        compiler_params=_params(("arbitrary",)),
        name="filter_time",
    )(zemb, w1p, b1p, w2p, b2p, f0p, f1p, w3p, deltas)


def _split_bf16(a):
    hi = a.astype(bf16)
    return hi, (a - hi.astype(f32)).astype(bf16)


def _dft_consts(p):
    q = p // 2
    theta = np.pi * (2 * np.arange(q) + 1) / (2 * p)
    te = theta[:, None] * (2 * np.arange(q))[None, :]
    to = theta[:, None] * (2 * np.arange(q) + 1)[None, :]
    fe = np.concatenate([np.cos(te), -np.sin(te)], axis=0)
    fo = np.concatenate([np.cos(to), -np.sin(to)], axis=0)
    ie = np.concatenate([np.cos(te).T, -np.sin(te).T], axis=1) / p
    io = np.concatenate([np.cos(to).T, -np.sin(to).T], axis=1) / p
    freq = np.concatenate([np.arange(q), p - 1 - np.arange(q)])
    tf = (np.pi * (2 * freq + 1) / (2 * p))[:, None] * np.arange(-p, p)[None, :]
    g = np.concatenate([np.cos(tf), -np.sin(tf)], axis=0)
    g[:, 0] = 0.0
    return tuple(jnp.asarray(a, bf16) for a in (fe, fo, ie, io, g))


def _filter_spec_kernel(g_ref, lo_ref, hi_ref, s_ref, ha_ref, hb_ref, *, p):
    h = (_dot(g_ref[:, :p], lo_ref[...].astype(bf16))
         + _dot(g_ref[:, p:], hi_ref[...].astype(bf16)))
    h = h / s_ref[...]
    ha_ref[0] = h[:p]
    hb_ref[0] = h[p:]


def _filter_spec(n, p, gmat, hraw, hsum):
    nb = n // p
    c2 = 2 * D
    tc = 512
    nd = 2 * nb - 1
    out = jax.ShapeDtypeStruct((nd, p, c2), f32)
    return pl.pallas_call(
        functools.partial(_filter_spec_kernel, p=p),
        grid=(nd, c2 // tc),
        in_specs=[
            _whole(gmat),
            pl.BlockSpec((p, tc), lambda d, c: (d, c)),
            pl.BlockSpec((p, tc), lambda d, c: (d + 1, c)),
            pl.BlockSpec((1, tc), lambda d, c: (0, c)),
        ],
        out_specs=[pl.BlockSpec((1, p, tc), lambda d, c: (d, 0, c))] * 2,
        out_shape=[out, out],
        compiler_params=_params(("arbitrary", "arbitrary")),
        name="filter_spec",
    )(gmat, hraw, hraw, hsum)


def _hyena_kernel(pb_hbm, cw_ref, cb_ref, skip_ref,
                  ha0_ref, hb0_ref, ha1_ref, hb1_ref, fe_ref, fo_ref, ie_ref, io_ref,
                  o_ref, ring_ref, sem_ref, sv_ref, s1_ref, s2_ref, u_ref, za_ref, zb_ref,
                  ye_ref, yo_ref, *, n, p):
    nb = n // p
    q = p // 2
    tc = HY_TC
    nct = D // tc
    nbat = o_ref.shape[0]
    halves = [slice(bb * tc, (bb + 1) * tc) for bb in range(nbat)]

    nbs = pl.num_programs(1)
    total = pl.num_programs(0) * nbs
    step = pl.program_id(0) * nbs + pl.program_id(1)

    def copies(s):
        slot = lax.rem(s, HY_RING)
        c, b0 = s // nbs, lax.rem(s, nbs) * nbat
        return [pltpu.make_async_copy(
            pb_hbm.at[pl.ds(b0, nbat), :, pl.ds(pl.multiple_of((part * nct + c) * tc, tc), tc)],
            ring_ref.at[slot, part], sem_ref.at[slot, part]) for part in range(3)]

    for ahead in range(HY_RING - 1):
        @pl.when((step == 0) & (ahead < total))
        def _():
            for cp in copies(jnp.int32(ahead)):
                cp.start()

    @pl.when(step + (HY_RING - 1) < total)
    def _():
        for cp in copies(step + (HY_RING - 1)):
            cp.start()

    for cp in copies(step):
        cp.wait()
    slot = lax.rem(step, HY_RING)
    pv_ref, p1_ref, p2_ref = (ring_ref.at[slot, part] for part in range(3))

    def short_conv(p_ref, part, dst_ref):
        w = cw_ref[part]
        w0, w1, w2, bias = w[0:1], w[1:2], w[2:3], cb_ref[part]
        edge = lax.broadcasted_iota(jnp.int32, (8, tc), 0)
        for bb in range(nbat):
            dst_ref[bb, 8:n - 8, :] = (p_ref[bb, 7:n - 9, :] * w0 + p_ref[bb, 8:n - 8, :] * w1
                                       + p_ref[bb, 9:n - 7, :] * w2 + bias)
            top = p_ref[bb, 0:8, :]
            prev = jnp.where(edge == 0, 0.0, pltpu.roll(top, 1, axis=0))
            dst_ref[bb, 0:8, :] = prev * w0 + top * w1 + p_ref[bb, 1:9, :] * w2 + bias
            bot = p_ref[bb, n - 8:n, :]
            nxt = jnp.where(edge == 7, 0.0, pltpu.roll(bot, 7, axis=0))
            dst_ref[bb, n - 8:n, :] = p_ref[bb, n - 9:n - 1, :] * w0 + bot * w1 + nxt * w2 + bias

    def samples(ref, start):
        rows = pl.ds(start, q, stride=2)
        return jnp.concatenate([ref.at[bb][rows, :] for bb in range(nbat)], axis=1)

    def forward(src_ref):
        for j in range(nb):
            e = _dot(fe_ref[...], samples(src_ref, j * p).astype(bf16))
            o = _dot(fo_ref[...], samples(src_ref, j * p + 1).astype(bf16))
            za_ref[j, 0:q] = e[:q] + o[:q]
            za_ref[j, q:p] = e[:q] - o[:q]
            zb_ref[j, 0:q] = e[q:] + o[q:]
            zb_ref[j, q:p] = o[q:] - e[q:]

    def filter_and_gate(src_ref, gate_ref, ha_ref, hb_ref, skip, dst_ref):
        for i in range(nb):
            for c in range(q // HY_ROWS):
                lo = slice(c * HY_ROWS, (c + 1) * HY_ROWS)
                hi = slice(q + c * HY_ROWS, q + (c + 1) * HY_ROWS)
                acc = []
                for rows in (lo, hi):
                    ya = [None] * nbat
                    yb = [None] * nbat
                    for j in range(nb):
                        d = i - j + (nb - 1)
                        h_a = ha_ref[d, rows, :]
                        h_b = hb_ref[d, rows, :]
                        for bb, lanes in enumerate(halves):
                            z_a = za_ref[j, rows, lanes]
                            z_b = zb_ref[j, rows, lanes]
                            t_a = h_a * z_a - h_b * z_b
                            t_b = h_a * z_b + h_b * z_a
                            ya[bb] = t_a if ya[bb] is None else ya[bb] + t_a
                            yb[bb] = t_b if yb[bb] is None else yb[bb] + t_b
                    acc.append((ya, yb))
                (ya_lo, yb_lo), (ya_hi, yb_hi) = acc
                for bb, lanes in enumerate(halves):
                    ye_ref[i, lo, lanes] = (ya_lo[bb] + ya_hi[bb]).astype(bf16)
                    ye_ref[i, hi, lanes] = (yb_lo[bb] - yb_hi[bb]).astype(bf16)
                    yo_ref[i, lo, lanes] = (ya_lo[bb] - ya_hi[bb]).astype(bf16)
                    yo_ref[i, hi, lanes] = (yb_lo[bb] + yb_hi[bb]).astype(bf16)
            for parity, y in ((0, _dot(ie_ref[...], ye_ref[i])), (1, _dot(io_ref[...], yo_ref[i]))):
                rows = pl.ds(i * p + parity, q, stride=2)
                for bb, lanes in enumerate(halves):
                    dst_ref.at[bb][rows, :] = gate_ref.at[bb][rows, :] * (
                        y[:, lanes] + src_ref.at[bb][rows, :] * skip)

    short_conv(pv_ref, 0, sv_ref)
    forward(sv_ref)
    short_conv(p1_ref, 1, s1_ref)
    short_conv(p2_ref, 2, s2_ref)
    filter_and_gate(sv_ref, s1_ref, ha0_ref, hb0_ref, skip_ref[0:1, :], u_ref)
    forward(u_ref)
    filter_and_gate(u_ref, s2_ref, ha1_ref, hb1_ref, skip_ref[1:2, :], sv_ref)
    for bb in range(nbat):
        o_ref[bb] = sv_ref[bb].astype(o_ref.dtype)


def _hyena(pb, conv_w3, conv_b3, skip, h_a, h_b, consts, p):
    b, n, _ = pb.shape
    tc = HY_TC
    nct = D // tc
    nb = n // p
    nd = 2 * nb - 1
    nbat = HY_NBAT[n]
    wide = nbat * tc

    def hspec(order):
        return pl.BlockSpec((nd, p, tc), lambda c, b: (0, 0, order * nct + c),
                            pipeline_mode=pl.Buffered(1))

    return pl.pallas_call(
        functools.partial(_hyena_kernel, n=n, p=p),
        grid=(nct, b // nbat),
        in_specs=[
            pl.BlockSpec(memory_space=pl.ANY),
            pl.BlockSpec((3, 3, tc), lambda c, b: (0, 0, c)),
            pl.BlockSpec((3, 1, tc), lambda c, b: (0, 0, c)),
            pl.BlockSpec((2, tc), lambda c, b: (0, c)),
            hspec(0), hspec(0), hspec(1), hspec(1),
        ] + [_whole(a) for a in consts],
        out_specs=pl.BlockSpec((nbat, n, tc), lambda c, b: (b, 0, c)),
        out_shape=jax.ShapeDtypeStruct((b, n, D), bf16),
        scratch_shapes=(
            [pltpu.VMEM((HY_RING, 3, nbat, n, tc), f32), pltpu.SemaphoreType.DMA((HY_RING, 3))]
            + [pltpu.VMEM((nbat, n, tc), f32)] * 4
            + [pltpu.VMEM((nb, p, wide), f32)] * 2
            + [pltpu.VMEM((nb, p, wide), bf16)] * 2
        ),
        compiler_params=_params(("arbitrary", "arbitrary")),
        name="hyena",
    )(pb, conv_w3, conv_b3, skip, h_a, h_b, h_a, h_b, *consts)


def _mix_ffn_kernel(*refs, na, final):
    x_ref, gate1_ref = refs[:2]
    a_refs = refs[2:2 + na]
    wo_ref, g_ref, sh_ref, sc_ref, gate2_ref, wgu_ref, wd_ref = refs[2 + na:9 + na]
    rest = refs[9 + na:]
    fg_ref = rest[0] if final else None
    o_ref, x1_ref = rest[-2:]

    mix = _dot(a_refs[0][0], wo_ref[0:D, :])
    for k in range(1, na):
        mix = mix + _dot(a_refs[k][0], wo_ref[k * D:(k + 1) * D, :])
    x1_ref[...] = x_ref[0] + gate1_ref[0] * mix

    h = _norm_mod(x1_ref[...], g_ref[...], sc_ref[0], sh_ref[0]).astype(bf16)
    acc = None
    for f in range(D_FF // FFN_TF):
        cols = slice(f * FFN_TF, (f + 1) * FFN_TF)
        ucols = slice(D_FF + f * FFN_TF, D_FF + (f + 1) * FFN_TF)
        act = (jax.nn.silu(_dot(h, wgu_ref[:, cols])) * _dot(h, wgu_ref[:, ucols])).astype(bf16)
        part = _dot(act, wd_ref[cols, :])
        acc = part if acc is None else acc + part
    y = x1_ref[...] + gate2_ref[0] * acc
    if final:
        y = y * lax.rsqrt(jnp.mean(y * y, axis=-1, keepdims=True) + EPS) * fg_ref[...]
    o_ref[0] = y


def _mix_ffn(x, mods3, row_of, acts, w_o, g, w_gu, w_down, tm, final_g=None):
    b, n, _ = x.shape
    na = len(acts)
    assert w_o[0].shape[1:] == (na * D, D) and all(a.shape[-1] == D for a in acts)
    extra = [] if final_g is None else [final_g]
    return pl.pallas_call(
        functools.partial(_mix_ffn_kernel, na=na, final=final_g is not None),
        grid=(b, n // tm),
        in_specs=([_row_spec(tm, D), _mod_spec(row_of, 2)]
                  + [_row_spec(tm, D) for _ in acts]
                  + [_layer_spec(w_o), _whole(g), _mod_spec(row_of, 3), _mod_spec(row_of, 4),
                     _mod_spec(row_of, 5), _layer_spec(w_gu), _layer_spec(w_down)]
                  + [_whole(a) for a in extra]),
        out_specs=_row_spec(tm, D),
        out_shape=jax.ShapeDtypeStruct((b, n, D), f32),
        scratch_shapes=[pltpu.VMEM((tm, D), f32)],
        compiler_params=_params(("arbitrary", "arbitrary")),
        name="mix_ffn",
    )(x, mods3, *acts, w_o[0], g, mods3, mods3, mods3, w_gu[0], w_down[0], *extra)


def _rope_tables(n):
    rows = n // GRID_W
    row = np.repeat(np.arange(rows), GRID_W).astype(np.float64)
    col = np.tile(np.arange(GRID_W), rows).astype(np.float64)
    half = HEAD_DIM // 2
    inv = ROPE_THETA ** (-np.arange(0, half, 2, dtype=np.float64) / half)
    ang = np.concatenate([row[:, None] * inv, col[:, None] * inv], axis=-1)
    cos, sin = np.cos(ang), np.sin(ang)
    return (jnp.asarray(np.concatenate([cos, cos], axis=-1), f32),
            jnp.asarray(np.concatenate([-sin, sin], axis=-1), f32))


def _qkv_kernel(*refs, with_q, rope):
    x_ref, g_ref, sh_ref, sc_ref, w_ref, qg_ref, kg_ref = refs[:7]
    pos = 7
    if rope:
        cos_ref, sin_ref = refs[7:9]
        pos = 9
    out_refs = refs[pos:]
    h = _norm_mod(x_ref[0], g_ref[...], sc_ref[0], sh_ref[0]).astype(bf16)
    qkv = _dot(h, w_ref[...] if with_q else w_ref[:, N_HEADS * HEAD_DIM:])
    ones = jnp.ones((HEAD_DIM, HEAD_DIM), bf16)

    def head(t, gain, scale):
        sq_hi, sq_lo = _split_bf16(t * t)
        ssq = _dot(sq_hi, ones) + _dot(sq_lo, ones)
        t = t * lax.rsqrt(ssq * (1.0 / HEAD_DIM) + EPS) * gain
        if rope:
            t = t * cos_ref[...] + pltpu.roll(t, HEAD_DIM // 2, axis=1) * sin_ref[...]
        return (t * scale).astype(bf16)

    off = 0
    if with_q:
        q_ref, k_ref, v_ref = out_refs
        for hd in range(N_HEADS):
            cols = slice(hd * HEAD_DIM, (hd + 1) * HEAD_DIM)
            q_ref[0, :, cols] = head(qkv[:, cols], qg_ref[...], HEAD_DIM ** -0.5)
        off = N_HEADS * HEAD_DIM
    else:
        k_ref, v_ref = out_refs
    for kv in range(N_KV):
        cols = slice(kv * HEAD_DIM, (kv + 1) * HEAD_DIM)
        src = slice(off + kv * HEAD_DIM, off + (kv + 1) * HEAD_DIM)
        k_ref[0, :, cols] = head(qkv[:, src], kg_ref[...], 1.0)
    voff = off + N_KV * HEAD_DIM
    for kv in range(N_KV):
        src = slice(voff + kv * HEAD_DIM, voff + (kv + 1) * HEAD_DIM)
        v_ref[0, :, 2 * kv * HEAD_DIM:(2 * kv + 1) * HEAD_DIM] = qkv[:, src].astype(bf16)
        v_ref[0, :, (2 * kv + 1) * HEAD_DIM:(2 * kv + 2) * HEAD_DIM] = jnp.ones(
            (qkv.shape[0], HEAD_DIM), bf16)


def _qkv(x, g, mods3, row_of, w, q_g, k_g, tables, with_q, tm):
    b, n, _ = x.shape
    nq = N_HEADS * HEAD_DIM
    nk = N_KV * HEAD_DIM
    rope = tables is not None
    in_specs = [_row_spec(tm, D), _whole(g), _mod_spec(row_of, 0), _mod_spec(row_of, 1),
                _layer_spec(w), _whole(q_g), _whole(k_g)]
    args = [x, g, mods3, mods3, w[0], q_g, k_g]
    if rope:
        in_specs += [pl.BlockSpec((tm, HEAD_DIM), lambda b, i: (i, 0))] * 2
        args += list(tables)
    out_specs = [_row_spec(tm, nk), _row_spec(tm, 2 * nk)]
    out_shape = [jax.ShapeDtypeStruct((b, n, nk), bf16), jax.ShapeDtypeStruct((b, n, 2 * nk), bf16)]
    if with_q:
        out_specs = [_row_spec(tm, nq)] + out_specs
        out_shape = [jax.ShapeDtypeStruct((b, n, nq), bf16)] + out_shape
    return pl.pallas_call(
        functools.partial(_qkv_kernel, with_q=with_q, rope=rope),
        grid=(b, n // tm),
        in_specs=in_specs,
        out_specs=out_specs,
        out_shape=out_shape,
        compiler_params=_params(("arbitrary", "arbitrary")),
        name="qkv",
    )(*args)


def _attn_kernel(*refs, nseg):
    q_ref = refs[0]
    k_refs = refs[1:1 + nseg]
    v_refs = refs[1 + nseg:1 + 2 * nseg]
    o_ref = refs[1 + 2 * nseg]
    def scores(hd):
        q = q_ref[0, :, hd * HEAD_DIM:(hd + 1) * HEAD_DIM]
        return [lax.dot_general(q, k[0], _NT, preferred_element_type=f32) for k in k_refs]

    s_next = scores(0)
    for hd in range(GQA):
        s = s_next
        if hd + 1 < GQA:
            s_next = scores(hd + 1)
        m = jnp.max(s[0], axis=-1, keepdims=True)
        for t in s[1:]:
            m = jnp.maximum(m, jnp.max(t, axis=-1, keepdims=True))
        ov = _dot(jnp.exp(s[0] - m).astype(bf16), v_refs[0][0])
        for t, v in zip(s[1:], v_refs[1:]):
            ov = ov + _dot(jnp.exp(t - m).astype(bf16), v[0])
        o_ref[0, :, hd * HEAD_DIM:(hd + 1) * HEAD_DIM] = (
            ov[:, :HEAD_DIM] / ov[:, HEAD_DIM:]).astype(o_ref.dtype)


def _attention(q, ks, vs, tq):
    b, n, _ = q.shape
    nseg = len(ks)
    gw = GQA * HEAD_DIM
    k_specs = [pl.BlockSpec((1, k.shape[1], HEAD_DIM), lambda b, h, i: (b, 0, h)) for k in ks]
    v_specs = [pl.BlockSpec((1, v.shape[1], 2 * HEAD_DIM), lambda b, h, i: (b, 0, h)) for v in vs]
    return pl.pallas_call(
        functools.partial(_attn_kernel, nseg=nseg),
        grid=(b, N_KV, n // tq),
        in_specs=[pl.BlockSpec((1, tq, gw), lambda b, h, i: (b, i, h))] + k_specs + v_specs,
        out_specs=pl.BlockSpec((1, tq, gw), lambda b, h, i: (b, i, h)),
        out_shape=jax.ShapeDtypeStruct((b, n, N_HEADS * HEAD_DIM), bf16),
        compiler_params=_params(("arbitrary", "arbitrary", "arbitrary")),
        name="attn",
    )(q, *ks, *vs)


def _pad2(a, rows, cols):
    return jnp.pad(a, ((0, rows - a.shape[0]), (0, cols - a.shape[1])))


def kernel(x, c, ctx, c_ctx, mod_w, mod_b, norm1_g, norm2_g, ffn_w_gu, ffn_w_down, even_w_in,
           gmlp_ln_g, gmlp_ln_b, gmlp_w_s, gmlp_b_s, hyena_conv_w, hyena_conv_b, hyena_f_w1,
           hyena_f_b1, hyena_f_w2, hyena_f_b2, hyena_f_w3, hyena_freq, hyena_skip, even_w_out,
           attn_w_qkv, attn_q_g, attn_k_g, attn_w_o, final_g):
    lat_row = lambda b: b
    ctx_row = lambda b: CTX_ROW
    streams = {"lat": (SEQ, lat_row, 512), "ctx": (CTX, ctx_row, 512)}

    cond = jnp.zeros((MOD_ROWS, D), f32).at[:NB].set(c).at[CTX_ROW].set(c_ctx)
    mods = _mods(cond, mod_w, mod_b)

    dft = {p: _dft_consts(p) for p in set(HY_P.values())}
    fconst = {n: _filter_consts(n) for n in (SEQ, CTX)}
    rope = _rope_tables(SEQ)
    nq = N_HEADS * HEAD_DIM
    nk = N_KV * HEAD_DIM

    w_gu_all, w_down_all = ffn_w_gu.astype(bf16), ffn_w_down.astype(bf16)
    w_in_all, w_out_all, w_s_all = (even_w_in.astype(bf16), even_w_out.astype(bf16),
                                    gmlp_w_s.astype(bf16))
    w_qkv_all, w_o_all = attn_w_qkv.astype(bf16), attn_w_o.astype(bf16)

    xs = {"lat": x, "ctx": ctx.reshape(1, NB * CTX, D)}
    for layer in range(DEPTH):
        last = layer == DEPTH - 1
        is_even = layer % 2 == 0
        mods3 = mods[layer].reshape(MOD_ROWS, 1, 6 * D)
        g1 = norm1_g[layer].reshape(1, D)
        g2 = norm2_g[layer].reshape(1, D)
        w_gu = (w_gu_all, layer)
        w_down = (w_down_all, layer)
        mixed = {}
        if is_even:
            i = layer // 2
            w_in = (w_in_all, i)
            w_out = (w_out_all, i)
            w_s = (w_s_all, i)
            b_s_rows = jnp.repeat(gmlp_b_s[i].T, CHUNK, axis=1)
            ln_g = gmlp_ln_g[i].reshape(1, D)
            ln_b = gmlp_ln_b[i].reshape(1, D)
            conv_w3 = hyena_conv_w[i].reshape(3, 3, D).transpose(1, 0, 2)
            conv_b3 = hyena_conv_b[i].reshape(3, 1, D)
            w1p = _pad2(hyena_f_w1[i], LANES, LANES)
            w2p = _pad2(hyena_f_w2[i], LANES, LANES)
            b1p = _pad2(hyena_f_b1[i][None, :], 1, LANES)
            b2p = _pad2(hyena_f_b2[i][None, :], 1, LANES)
            f0p = _pad2(hyena_freq[i, 0][None, :], 1, LANES)
            f1p = _pad2(hyena_freq[i, 1][None, :], 1, LANES)
            w3p = jnp.pad(hyena_f_w3[i].reshape(FILTER_HIDDEN, 2, 2 * D).transpose(1, 0, 2),
                          ((0, 0), (0, LANES - FILTER_HIDDEN), (0, 0)))
            for key, (n, row_of, tm) in streams.items():
                if key == "ctx" and last:
                    continue
                xc = xs[key]
                p = HY_P[n]
                *hy_consts, gmat = dft[p]
                zemb, deltas = fconst[n]
                hraw, hsum = _filter_time(n, zemb, deltas, w1p, b1p, w2p, b2p, f0p, f1p, w3p)
                h_a, h_b = _filter_spec(n, p, gmat, hraw, hsum)
                y_a, pb = _even_in(xc, g1, mods3, row_of, w_in, ln_g, ln_b, w_s, b_s_rows, tm)
                z_b = _hyena(pb.reshape(NB, n, 3 * D), conv_w3, conv_b3, hyena_skip[i],
                             h_a, h_b, hy_consts, p)
                mixed[key] = ([y_a, z_b.reshape(y_a.shape)], w_out)
        else:
            j = layer // 2
            w_qkv = (w_qkv_all, j)
            w_o = (w_o_all, j)
            q_g = attn_q_g[j].reshape(1, HEAD_DIM)
            k_g = attn_k_g[j].reshape(1, HEAD_DIM)
            q_l, k_l, v_l = _qkv(xs["lat"], g1, mods3, lat_row, w_qkv, q_g, k_g, rope, True, 1024)
            if last:
                k_c, v_c = _qkv(xs["ctx"], g1, mods3, ctx_row, w_qkv, q_g, k_g, None, False, 512)
            else:
                q_c, k_c, v_c = _qkv(xs["ctx"], g1, mods3, ctx_row, w_qkv, q_g, k_g, None, True, 512)
            k_c = k_c.reshape(NB, CTX, nk)
            v_c = v_c.reshape(NB, CTX, 2 * nk)
            mixed["lat"] = ([_attention(q_l, [k_c, k_l], [v_c, v_l], 1024)], w_o)
            if not last:
                o_c = _attention(q_c.reshape(NB, CTX, nq), [k_c], [v_c], CTX)
                mixed["ctx"] = ([o_c.reshape(1, NB * CTX, nq)], w_o)
        for key, (acts, w_mix) in mixed.items():
            _, row_of, tm = streams[key]
            fg = final_g.reshape(1, D) if last else None
            xs[key] = _mix_ffn(xs[key], mods3, row_of, acts, w_mix, g2, w_gu, w_down, tm, fg)
    return xs["lat"]
```
